```python
import math
import jax, jax.numpy as jnp
from jax import lax
import numpy as np

D_MODEL = 1024
BATCH = 4
SEQ = 4096
DEPTH = 4

HEAD_DIM = 64
SWA_HEADS = 8
SWA_KV_HEADS = 2
SWA_GROUP = SWA_HEADS // SWA_KV_HEADS
WINDOW = 128
DIFF_HEADS = 4
DIFF_VDIM = 2 * HEAD_DIM
BLOCK = 128
D_FF = 2816
CONV_WIDTH = 3
RMS_EPS = 1e-6
NEG_INF = -1e30

SWA_Q = SWA_HEADS * HEAD_DIM
SWA_KV = SWA_KV_HEADS * HEAD_DIM
DIFF_QK = DIFF_HEADS * 2 * HEAD_DIM
DIFF_V = DIFF_HEADS * DIFF_VDIM
IN_COLS = SWA_Q + 2 * SWA_KV + 2 * DIFF_QK + DIFF_V + 2 * D_MODEL

kernel_name = "hybrid_swa_sink_diffattn_alibi_convffn_adaln"


def _rmsnorm(x, g):
    xf = x.astype(jnp.float32)
    y = xf * lax.rsqrt(jnp.mean(xf * xf, axis=-1, keepdims=True) + RMS_EPS)
    return (y * g.astype(jnp.float32)).astype(x.dtype)


def _alibi_slopes(n):
    return jnp.array([2.0 ** (-8.0 * (i + 1) / n) for i in range(n)], dtype=jnp.float32)


def _sliding_window_attention(q, k, v, sinks, slopes):
    b, s, _, d = q.shape
    nb = s // BLOCK
    qb = q.reshape(b, nb, BLOCK, SWA_KV_HEADS, SWA_GROUP, d)

    def with_prev(t):
        tb = t.reshape(b, nb, BLOCK, SWA_KV_HEADS, d)
        prev = jnp.concatenate([jnp.zeros_like(tb[:, :1]), tb[:, :-1]], axis=1)
        return jnp.concatenate([prev, tb], axis=2)

    kb, vb = with_prev(k), with_prev(v)
    scores = jnp.einsum('bnqkgd,bnskd->bnkgqs', qb, kb).astype(jnp.float32) * (d ** -0.5)
    qi = jnp.arange(BLOCK)[:, None]
    kj = jnp.arange(2 * BLOCK)[None, :]
    delta = qi + BLOCK - kj
    key_pos = jnp.arange(nb)[:, None, None] * BLOCK - BLOCK + kj[None]
    valid = (delta >= 0) & (delta < WINDOW) & (key_pos >= 0)
    bias = -slopes[:, :, None, None] * delta.astype(jnp.float32)
    scores = jnp.where(valid[None, :, None, None], scores + bias, NEG_INF)
    sink = jnp.broadcast_to(sinks.astype(jnp.float32).reshape(SWA_KV_HEADS, SWA_GROUP, 1, 1),
                            scores.shape[:-1] + (1,))
    probs = jax.nn.softmax(jnp.concatenate([scores, sink], axis=-1), axis=-1)[..., :-1]
    out = jnp.einsum('bnkgqs,bnskd->bnqkgd', probs.astype(v.dtype), vb)
    return out.reshape(b, s, SWA_HEADS * d)


def _diff_attention(q, k, v, lam, lam_init, subln_g, slopes):
    b, s = q.shape[:2]
    nb = s // BLOCK
    qb = jnp.moveaxis(q.reshape(b, nb, BLOCK, DIFF_HEADS, 2, HEAD_DIM), 1, 0)
    key_pos = jnp.arange(s)
    scale = HEAD_DIM ** -0.5

    def block_fn(args):
        qblk, n = args
        scores = jnp.einsum('bqhmd,bshmd->bhmqs', qblk, k).astype(jnp.float32) * scale
        q_pos = n * BLOCK + jnp.arange(BLOCK)
        delta = q_pos[:, None] - key_pos[None, :]
        scores = scores - slopes[:, None, None, None] * delta.astype(jnp.float32)
        scores = jnp.where(delta >= 0, scores, NEG_INF)
        p = jax.nn.softmax(scores, axis=-1)
        a = p[:, :, 0] - lam * p[:, :, 1]
        return jnp.einsum('bhqs,bshe->bqhe', a.astype(v.dtype), v)

    out = lax.map(block_fn, (qb, jnp.arange(nb)))
    out = jnp.moveaxis(out, 0, 1).reshape(b, s, DIFF_HEADS, DIFF_VDIM)
    out = _rmsnorm(out, subln_g) * (1.0 - lam_init)
    return out.reshape(b, s, DIFF_V)


def _conv_ffn(h, w_up, conv_w, conv_b, w_down):
    u = h @ w_up
    a, g = jnp.split(u, 2, axis=-1)
    s = a.shape[1]
    ap = jnp.pad(a, ((0, 0), (CONV_WIDTH - 1, 0), (0, 0)))
    a = ap[:, 0:s] * conv_w[0] + ap[:, 1:s + 1] * conv_w[1] + ap[:, 2:s + 2] * conv_w[2] + conv_b
    return (jax.nn.gelu(a, approximate=False) * g) @ w_down


def setup_inputs(seed: int = 0) -> dict:
    key = jax.random.key(seed)
    ks = jax.random.split(key, 20)
    nrm = lambda k, shape, s: jax.random.normal(k, shape, jnp.float32) * s
    D = D_MODEL
    return {
        "x": nrm(ks[0], (BATCH, SEQ, D), 1.0),
        "c": nrm(ks[1], (BATCH, D), 1.0),
        "w_ada": nrm(ks[2], (DEPTH, D, 6 * D), 0.5 * D ** -0.5),
        "b_ada": nrm(ks[3], (DEPTH, 6 * D), 0.02),
        "norm1_g": 1.0 + nrm(ks[4], (DEPTH, D), 0.02),
        "w_in": nrm(ks[5], (DEPTH, D, IN_COLS), D ** -0.5),
        "swa_sinks": nrm(ks[6], (DEPTH, SWA_HEADS), 1.0),
        "diff_lambda": nrm(ks[7], (DEPTH, 4, HEAD_DIM), 0.1),
        "diff_subln_g": 1.0 + nrm(ks[8], (DEPTH, DIFF_VDIM), 0.02),
        "w_branch_swa": nrm(ks[9], (DEPTH, SWA_Q, D), SWA_Q ** -0.5),
        "w_branch_diff": nrm(ks[10], (DEPTH, DIFF_V, D), DIFF_V ** -0.5),
        "w_out": nrm(ks[11], (DEPTH, D, D), D ** -0.5),
        "norm2_g": 1.0 + nrm(ks[12], (DEPTH, D), 0.02),
        "w_up": nrm(ks[13], (DEPTH, D, 2 * D_FF), D ** -0.5),
        "conv_w": nrm(ks[14], (DEPTH, CONV_WIDTH, D_FF), CONV_WIDTH ** -0.5),
        "conv_b": nrm(ks[15], (DEPTH, D_FF), 0.02),
        "w_down": nrm(ks[16], (DEPTH, D_FF, D), D_FF ** -0.5),
        "final_g": 1.0 + nrm(ks[17], (D,), 0.02),
    }


def reference(x, c, w_ada, b_ada, norm1_g, w_in, swa_sinks, diff_lambda, diff_subln_g,
              w_branch_swa, w_branch_diff, w_out, norm2_g, w_up, conv_w, conv_b, w_down, final_g):
    b, s, _ = x.shape
    sizes = [SWA_Q, SWA_KV, SWA_KV, DIFF_QK, DIFF_QK, DIFF_V, D_MODEL, D_MODEL]
    points = [int(p) for p in np.cumsum(sizes)[:-1]]
    slopes_swa = _alibi_slopes(SWA_HEADS).reshape(SWA_KV_HEADS, SWA_GROUP)
    slopes_diff = _alibi_slopes(DIFF_HEADS)
    cs = jax.nn.silu(c)
    for l in range(DEPTH):
        mod = cs @ w_ada[l] + b_ada[l]
        shift1, scale1, gate1, shift2, scale2, gate2 = [m[:, None, :] for m in jnp.split(mod, 6, axis=-1)]

        h = _rmsnorm(x, norm1_g[l]) * (1.0 + scale1) + shift1
        proj = h @ w_in[l]
        qa, ka, va, qd, kd, vd, ga, gd = jnp.split(proj, points, axis=-1)
        out_a = _sliding_window_attention(
            qa.reshape(b, s, SWA_HEADS, HEAD_DIM),
            ka.reshape(b, s, SWA_KV_HEADS, HEAD_DIM),
            va.reshape(b, s, SWA_KV_HEADS, HEAD_DIM),
            swa_sinks[l], slopes_swa)
        lam_init = 0.8 - 0.6 * math.exp(-0.3 * l)
        lp = diff_lambda[l].astype(jnp.float32)
        lam = jnp.exp(jnp.sum(lp[0] * lp[1])) - jnp.exp(jnp.sum(lp[2] * lp[3])) + lam_init
        out_d = _diff_attention(
            qd.reshape(b, s, DIFF_HEADS, 2, HEAD_DIM),
            kd.reshape(b, s, DIFF_HEADS, 2, HEAD_DIM),
            vd.reshape(b, s, DIFF_HEADS, DIFF_VDIM),
            lam, lam_init, diff_subln_g[l], slopes_diff)
        merged = (jax.nn.sigmoid(ga) * (out_a @ w_branch_swa[l])
                  + jax.nn.sigmoid(gd) * (out_d @ w_branch_diff[l]))
        x = x + gate1 * (merged @ w_out[l])

        h2 = _rmsnorm(x, norm2_g[l]) * (1.0 + scale2) + shift2
        x = x + gate2 * _conv_ffn(h2, w_up[l], conv_w[l], conv_b[l], w_down[l])
    return _rmsnorm(x, final_g)
```

```python
import functools
import math

import jax
import jax.numpy as jnp
import numpy as np
from jax import lax
from jax.experimental import pallas as pl
from jax.experimental.pallas import tpu as pltpu

F32 = jnp.float32
BF16 = jnp.bfloat16

HEAD_DIM = 64
SWA_HEADS = 8
SWA_KV_HEADS = 2
SWA_GROUP = SWA_HEADS // SWA_KV_HEADS
WINDOW = 128
BLOCK = 128
DIFF_HEADS = 4
DIFF_VDIM = 2 * HEAD_DIM
CONV_WIDTH = 3
RMS_EPS = 1e-6
NEG_INF = -1e30

LANES = 128
VMEM_LIMIT_BYTES = 56 * 1024 * 1024

TM = 512
SWA_TQ = 512
DIFF_TQ = 256
DIFF_TK = 256
FF_CHUNKS = (768, 768, 768, 512)

NT_DIMS = (((1,), (1,)), ((), ()))


def _params():
    return pltpu.CompilerParams(vmem_limit_bytes=VMEM_LIMIT_BYTES)


def _alibi_slopes(n):
    return [2.0 ** (-8.0 * (i + 1) / n) for i in range(n)]


def _ada_kernel(c_ref, w_ref, b_ref, o_ref):
    c = c_ref[...]
    cs = c * jax.nn.sigmoid(c)
    o_ref[0] = jnp.dot(cs, w_ref[0], preferred_element_type=F32,
                       precision=lax.Precision.HIGHEST) + b_ref[0]


def _ada(c8, w_ada, b_ada):
    depth, d, six_d = w_ada.shape
    tn = 1024
    return pl.pallas_call(
        _ada_kernel,
        grid=(depth, six_d // tn),
        in_specs=[
            pl.BlockSpec((8, d), lambda l, j: (0, 0)),
            pl.BlockSpec((1, d, tn), lambda l, j: (l, 0, j)),
            pl.BlockSpec((1, 1, tn), lambda l, j: (l, 0, j)),
        ],
        out_specs=pl.BlockSpec((1, 8, tn), lambda l, j: (l, 0, j)),
        out_shape=jax.ShapeDtypeStruct((depth, 8, six_d), F32),
        compiler_params=_params(),
        name="ada_mod",
    )(c8, w_ada, b_ada.reshape(depth, 1, six_d))


IN_GROUPS = (("qa", 512), ("kax", 512), ("vax", 256), ("qd", 512), ("kd", 512), ("vd", 512), ("g", 2048))


def _in_proj_kernel(x_ref, mod_ref, g_ref, w_ref, qa_ref, kax_ref, vax_ref, qd_ref, kd_ref, vd_ref, sg_ref):
    x = x_ref[0]
    ms = jnp.mean(x * x, axis=-1, keepdims=True)
    y = x * lax.rsqrt(ms + RMS_EPS) * g_ref[...]
    h = y * (1.0 + mod_ref[0, 1:2, :]) + mod_ref[0, 0:1, :]
    hb = h.astype(BF16)
    outs = dict(qa=qa_ref, kax=kax_ref, vax=vax_ref, qd=qd_ref, kd=kd_ref, vd=vd_ref, g=sg_ref)
    c0 = 0
    for name, width in IN_GROUPS:
        p = jnp.dot(hb, w_ref[:, c0:c0 + width], preferred_element_type=F32)
        if name in ("qa", "qd"):
            p = p * (HEAD_DIM ** -0.5)
        elif name == "g":
            p = jax.nn.sigmoid(p)
        outs[name][0] = p.astype(BF16)
        c0 += width


def _in_proj(x, mod_l, norm_g, w_in_x):
    b, s, d = x.shape
    widths = [w for _, w in IN_GROUPS]
    total = sum(widths)
    return pl.pallas_call(
        _in_proj_kernel,
        grid=(b, s // TM),
        in_specs=[
            pl.BlockSpec((1, TM, d), lambda bi, i: (bi, i, 0)),
            pl.BlockSpec((1, 6, d), lambda bi, i: (bi, 0, 0)),
            pl.BlockSpec((1, d), lambda bi, i: (0, 0)),
            pl.BlockSpec((d, total), lambda bi, i: (0, 0)),
        ],
        out_specs=[pl.BlockSpec((1, TM, w), lambda bi, i: (bi, i, 0)) for w in widths],
        out_shape=[jax.ShapeDtypeStruct((b, s, w), BF16) for w in widths],
        compiler_params=_params(),
        name="in_proj",
    )(x, mod_l, norm_g.reshape(1, d), w_in_x)


def _swa_kernel(sink_ref, q_ref, kc_ref, kp_ref, vc_ref, vp_ref, bias_ref, o_ref, kcat, vcat):
    first = pl.program_id(1) == 0
    kcat[0:BLOCK] = kp_ref[0]
    kcat[BLOCK:] = kc_ref[0]
    vcat[0:BLOCK] = vp_ref[0]
    vcat[BLOCK:] = vc_ref[0]
    col = lax.broadcasted_iota(jnp.int32, (BLOCK, 2 * BLOCK), 1)
    lane = lax.broadcasted_iota(jnp.int32, (BLOCK, LANES), 1)
    for sb in range(SWA_TQ // BLOCK):
        r0 = sb * BLOCK
        for t in range(SWA_HEADS // 2):
            kh = t // (SWA_GROUP // 2)
            q_t = q_ref[0, r0:r0 + BLOCK, t * LANES:(t + 1) * LANES]
            v = vcat[r0:r0 + 2 * BLOCK, kh * LANES:(kh + 1) * LANES]
            halves = []
            for e in range(2):
                hq = 2 * t + e
                kx = kcat[r0:r0 + 2 * BLOCK, (2 * kh + e) * LANES:(2 * kh + e + 1) * LANES]
                s = lax.dot_general(q_t, kx, NT_DIMS, preferred_element_type=F32)
                bias = bias_ref[hq]
                valid = bias > 0.5 * NEG_INF
                if sb == 0:
                    valid = valid & jnp.logical_not(first & (col < BLOCK))
                s = jnp.where(valid, s + bias, NEG_INF)
                sink = sink_ref[hq]
                m = jnp.maximum(jnp.max(s, axis=-1, keepdims=True), sink)
                p = jnp.exp(s - m)
                denom = jnp.sum(p, axis=-1, keepdims=True) + jnp.exp(sink - m)
                o = jnp.dot(p.astype(BF16), v, preferred_element_type=F32)
                halves.append(o / denom)
            o_t = jnp.where(lane < HEAD_DIM, halves[0], halves[1])
            o_ref[0, r0:r0 + BLOCK, t * LANES:(t + 1) * LANES] = o_t.astype(BF16)


def _swa_bias_table():
    slopes = _alibi_slopes(SWA_HEADS)
    qi = np.arange(BLOCK)[:, None]
    kj = np.arange(2 * BLOCK)[None, :]
    delta = qi + BLOCK - kj
    valid = (delta >= 0) & (delta < WINDOW)
    tab = np.stack([np.where(valid, -np.float32(sl) * delta.astype(np.float32), np.float32(NEG_INF))
                    for sl in slopes]).astype(np.float32)
    return jnp.asarray(tab)


def _swa(qa, kax, vax, sinks, bias):
    b, s, _ = qa.shape
    blocks_per_step = SWA_TQ // BLOCK
    prev_map = lambda bi, i: (bi, jnp.maximum(i * blocks_per_step - 1, 0), 0)
    cur_map = lambda bi, i: (bi, i, 0)
    return pl.pallas_call(
        _swa_kernel,
        grid=(b, s // SWA_TQ),
        in_specs=[
            pl.BlockSpec(memory_space=pltpu.SMEM),
            pl.BlockSpec((1, SWA_TQ, 512), cur_map),
            pl.BlockSpec((1, SWA_TQ, 512), cur_map),
            pl.BlockSpec((1, BLOCK, 512), prev_map),
            pl.BlockSpec((1, SWA_TQ, 256), cur_map),
            pl.BlockSpec((1, BLOCK, 256), prev_map),
            pl.BlockSpec((SWA_HEADS, BLOCK, 2 * BLOCK), lambda bi, i: (0, 0, 0)),
        ],
        out_specs=pl.BlockSpec((1, SWA_TQ, 512), cur_map),
        out_shape=jax.ShapeDtypeStruct((b, s, 512), BF16),
        scratch_shapes=[pltpu.VMEM((SWA_TQ + BLOCK, 512), BF16), pltpu.VMEM((SWA_TQ + BLOCK, 256), BF16)],
        compiler_params=_params(),
        name="swa_attn",
    )(sinks, qa, kax, kax, vax, vax, bias)


def _diff_kernel(sc_ref, q_ref, k_ref, v_ref, bias_ref, lam_ref, g_ref, o_ref,
                 acc1, acc2, m1, m2, l1, l2):
    h = pl.program_id(1)
    i = pl.program_id(2)
    slope = sc_ref[h]
    lam_init = sc_ref[DIFF_HEADS]
    q = q_ref[0]
    bias = bias_ref[0]
    lane = lax.broadcasted_iota(jnp.int32, (DIFF_TK, LANES), 1)
    low = lane < HEAD_DIM

    acc1[...] = jnp.zeros_like(acc1)
    acc2[...] = jnp.zeros_like(acc2)
    m1[...] = jnp.full_like(m1, NEG_INF)
    m2[...] = jnp.full_like(m2, NEG_INF)
    l1[...] = jnp.zeros_like(l1)
    l2[...] = jnp.zeros_like(l2)

    def update(s, v, cj, acc, m, l):
        m_old = m[...]
        m_new = jnp.maximum(m_old, jnp.max(s, axis=-1, keepdims=True) + cj)
        alpha = jnp.exp(m_old - m_new)
        p = jnp.exp(s - (m_new - cj))
        l[...] = alpha * l[...] + jnp.sum(p, axis=-1, keepdims=True)
        acc[...] = alpha * acc[...] + jnp.dot(p.astype(BF16), v, preferred_element_type=F32)
        m[...] = m_new

    def tile(j, causal):
        k0 = pl.multiple_of(j * DIFF_TK, DIFF_TK)
        k = k_ref[0, pl.ds(k0, DIFF_TK), :]
        v = v_ref[0, pl.ds(k0, DIFF_TK), :]
        zero = jnp.zeros_like(k)
        k1 = jnp.where(low, k, zero)
        k2 = jnp.where(low, zero, k)
        s1 = lax.dot_general(q, k1, NT_DIMS, preferred_element_type=F32) + bias
        s2 = lax.dot_general(q, k2, NT_DIMS, preferred_element_type=F32) + bias
        if causal:
            row = lax.broadcasted_iota(jnp.int32, (DIFF_TQ, DIFF_TK), 0)
            colk = lax.broadcasted_iota(jnp.int32, (DIFF_TQ, DIFF_TK), 1)
            keep = colk <= row
            s1 = jnp.where(keep, s1, NEG_INF)
            s2 = jnp.where(keep, s2, NEG_INF)
        cj = -slope * ((i - j) * DIFF_TQ).astype(F32)
        update(s1, v, cj, acc1, m1, l1)
        update(s2, v, cj, acc2, m2, l2)

    def body(j, carry):
        tile(j, False)
        return carry

    lax.fori_loop(0, i, body, 0)
    tile(i, True)

    lp = lam_ref[...]
    lam = (jnp.exp(jnp.sum(lp[0:1] * lp[1:2], axis=-1, keepdims=True))
           - jnp.exp(jnp.sum(lp[2:3] * lp[3:4], axis=-1, keepdims=True)) + lam_init)
    o = acc1[...] / l1[...] - lam * (acc2[...] / l2[...])
    ms = jnp.mean(o * o, axis=-1, keepdims=True)
    y = o * lax.rsqrt(ms + RMS_EPS) * g_ref[...]
    o_ref[0] = (y * (1.0 - lam_init)).astype(BF16)


def _diff_bias_table():
    slopes = _alibi_slopes(DIFF_HEADS)
    r = np.arange(DIFF_TQ)[:, None]
    c = np.arange(DIFF_TK)[None, :]
    rc = (r - c).astype(np.float32)
    return jnp.asarray(np.stack([-np.float32(sl) * rc for sl in slopes]).astype(np.float32))


def _diff(qd, kd, vd, scalars, bias, lam_p, subln_g):
    b, s, _ = qd.shape
    q_map = lambda bi, h, i: (bi, i, h)
    kv_map = lambda bi, h, i: (bi, 0, h)
    return pl.pallas_call(
        _diff_kernel,
        grid=(b, DIFF_HEADS, s // DIFF_TQ),
        in_specs=[
            pl.BlockSpec(memory_space=pltpu.SMEM),
            pl.BlockSpec((1, DIFF_TQ, LANES), q_map),
            pl.BlockSpec((1, s, LANES), kv_map),
            pl.BlockSpec((1, s, LANES), kv_map),
            pl.BlockSpec((1, DIFF_TQ, DIFF_TK), lambda bi, h, i: (h, 0, 0)),
            pl.BlockSpec((4, HEAD_DIM), lambda bi, h, i: (0, 0)),
            pl.BlockSpec((1, DIFF_VDIM), lambda bi, h, i: (0, 0)),
        ],
        out_specs=pl.BlockSpec((1, DIFF_TQ, LANES), q_map),
        out_shape=jax.ShapeDtypeStruct((b, s, DIFF_HEADS * DIFF_VDIM), BF16),
        scratch_shapes=[pltpu.VMEM((DIFF_TQ, DIFF_VDIM), F32), pltpu.VMEM((DIFF_TQ, DIFF_VDIM), F32)]
                       + [pltpu.VMEM((DIFF_TQ, 1), F32)] * 4,
        compiler_params=_params(),
        name="diff_attn",
    )(scalars, qd, kd, vd, bias, lam_p, subln_g.reshape(1, DIFF_VDIM))


def _merge_kernel(oa_ref, od_ref, sg_ref, x_ref, mod_ref, wbs_ref, wbd_ref, wo_ref, o_ref):
    d = x_ref.shape[-1]
    pa = jnp.dot(oa_ref[0], wbs_ref[...], preferred_element_type=F32)
    pd = jnp.dot(od_ref[0], wbd_ref[...], preferred_element_type=F32)
    merged = sg_ref[0, :, :d].astype(F32) * pa + sg_ref[0, :, d:].astype(F32) * pd
    y = jnp.dot(merged.astype(BF16), wo_ref[...], preferred_element_type=F32)
    o_ref[0] = x_ref[0] + mod_ref[0, 2:3, :] * y


def _merge(out_a, out_d, sg, x, mod_l, w_bs, w_bd, w_o):
    b, s, d = x.shape
    row = lambda bi, i: (bi, i, 0)
    const = lambda bi, i: (0, 0)
    return pl.pallas_call(
        _merge_kernel,
        grid=(b, s // TM),
        in_specs=[
            pl.BlockSpec((1, TM, out_a.shape[-1]), row),
            pl.BlockSpec((1, TM, out_d.shape[-1]), row),
            pl.BlockSpec((1, TM, 2 * d), row),
            pl.BlockSpec((1, TM, d), row),
            pl.BlockSpec((1, 6, d), lambda bi, i: (bi, 0, 0)),
            pl.BlockSpec(w_bs.shape, const),
            pl.BlockSpec(w_bd.shape, const),
            pl.BlockSpec(w_o.shape, const),
        ],
        out_specs=pl.BlockSpec((1, TM, d), row),
        out_shape=jax.ShapeDtypeStruct((b, s, d), F32),
        compiler_params=_params(),
        name="merge_out",
    )(out_a, out_d, sg, x, mod_l, w_bs, w_bd, w_o)


def _ffn_kernel(x_ref, mod_ref, g_ref, wu_ref, cw_ref, cb_ref, wd_ref, fg_ref, o_ref, act, carry, *, final):
    d_ff = wd_ref.shape[0]
    tm = x_ref.shape[1]

    @pl.when(pl.program_id(1) == 0)
    def _():
        carry[...] = jnp.zeros_like(carry)

    x = x_ref[0]
    ms = jnp.mean(x * x, axis=-1, keepdims=True)
    y = x * lax.rsqrt(ms + RMS_EPS) * g_ref[...]
    h = y * (1.0 + mod_ref[0, 4:5, :]) + mod_ref[0, 3:4, :]
    hb = h.astype(BF16)

    c0 = 0
    for width in FF_CHUNKS:
        a = jnp.dot(hb, wu_ref[:, c0:c0 + width], preferred_element_type=F32)
        gate = jnp.dot(hb, wu_ref[:, d_ff + c0:d_ff + c0 + width], preferred_element_type=F32)
        full = jnp.concatenate([carry[:, c0:c0 + width], a], axis=0)
        carry[:, c0:c0 + width] = a[tm - 8:, :]
        a1 = pltpu.roll(full, 1, 0)[8:]
        a2 = pltpu.roll(full, 2, 0)[8:]
        z = (a2 * cw_ref[0:1, c0:c0 + width] + a1 * cw_ref[1:2, c0:c0 + width]
             + a * cw_ref[2:3, c0:c0 + width] + cb_ref[:, c0:c0 + width])
        gelu = 0.5 * z * (1.0 + lax.erf(z * (2.0 ** -0.5)))
        act[:, c0:c0 + width] = (gelu * gate).astype(BF16)
        c0 += width

    yd = jnp.dot(act[...], wd_ref[...], preferred_element_type=F32)
    out = x + mod_ref[0, 5:6, :] * yd
    if final:
        ms2 = jnp.mean(out * out, axis=-1, keepdims=True)
        out = out * lax.rsqrt(ms2 + RMS_EPS) * fg_ref[...]
    o_ref[0] = out


def _ffn(x, mod_l, norm_g, w_up, conv_w, conv_b, w_down, final_g, final):
    b, s, d = x.shape
    d_ff = w_down.shape[0]
    row = lambda bi, i: (bi, i, 0)
    const = lambda bi, i: (0, 0)
    return pl.pallas_call(
        functools.partial(_ffn_kernel, final=final),
        grid=(b, s // TM),
        in_specs=[
            pl.BlockSpec((1, TM, d), row),
            pl.BlockSpec((1, 6, d), lambda bi, i: (bi, 0, 0)),
            pl.BlockSpec((1, d), const),
            pl.BlockSpec(w_up.shape, const),
            pl.BlockSpec((CONV_WIDTH, d_ff), const),
            pl.BlockSpec((1, d_ff), const),
            pl.BlockSpec(w_down.shape, const),
            pl.BlockSpec((1, d), const),
        ],
        out_specs=pl.BlockSpec((1, TM, d), row),
        out_shape=jax.ShapeDtypeStruct((b, s, d), F32),
        scratch_shapes=[pltpu.VMEM((TM, d_ff), BF16), pltpu.VMEM((8, d_ff), F32)],
        compiler_params=_params(),
        name="conv_ffn",
    )(x, mod_l, norm_g.reshape(1, d), w_up, conv_w, conv_b.reshape(1, d_ff), w_down, final_g.reshape(1, d))


def _expand_in_weight(w):
    d = w.shape[0]
    qa, ka, va, rest = w[:, :512], w[:, 512:640], w[:, 640:768], w[:, 768:]
    z = jnp.zeros((d, HEAD_DIM), w.dtype)
    kax, vax = [], []
    for kh in range(SWA_KV_HEADS):
        k = ka[:, kh * HEAD_DIM:(kh + 1) * HEAD_DIM]
        v = va[:, kh * HEAD_DIM:(kh + 1) * HEAD_DIM]
        kax += [k, z, z, k]
        vax += [v, v]
    return jnp.concatenate([qa] + kax + vax + [rest], axis=1).astype(BF16)


def kernel(x, c, w_ada, b_ada, norm1_g, w_in, swa_sinks, diff_lambda, diff_subln_g, w_branch_swa,
           w_branch_diff, w_out, norm2_g, w_up, conv_w, conv_b, w_down, final_g):
    b, s, d = x.shape
    depth = w_ada.shape[0]

    c8 = jnp.zeros((8, d), F32).at[:b].set(c)
    mod = _ada(c8, w_ada, b_ada)[:, :b].reshape(depth, b, 6, d)

    swa_bias = _swa_bias_table()
    diff_bias = _diff_bias_table()
    diff_slopes = _alibi_slopes(DIFF_HEADS)

    for l in range(depth):
        lam_init = 0.8 - 0.6 * math.exp(-0.3 * l)
        qa, kax, vax, qd, kd, vd, sg = _in_proj(x, mod[l], norm1_g[l], _expand_in_weight(w_in[l]))
        out_a = _swa(qa, kax, vax, swa_sinks[l], swa_bias)
        scalars = jnp.asarray(diff_slopes + [lam_init], F32)
        out_d = _diff(qd, kd, vd, scalars, diff_bias, diff_lambda[l], diff_subln_g[l])
        x = _merge(out_a, out_d, sg, x, mod[l], w_branch_swa[l].astype(BF16), w_branch_diff[l].astype(BF16),
                   w_out[l].astype(BF16))
        x = _ffn(x, mod[l], norm2_g[l], w_up[l].astype(BF16), conv_w[l], conv_b[l], w_down[l].astype(BF16),
                 final_g, final=(l == depth - 1))
    return x
```

```python
import functools
import math

import jax
import jax.numpy as jnp
import numpy as np
from jax import lax
from jax.experimental import pallas as pl
from jax.experimental.pallas import tpu as pltpu

F32 = jnp.float32
BF16 = jnp.bfloat16

HEAD_DIM = 64
SWA_HEADS = 8
SWA_KV_HEADS = 2
SWA_GROUP = SWA_HEADS // SWA_KV_HEADS
WINDOW = 128
BLOCK = 128
DIFF_HEADS = 4
DIFF_VDIM = 2 * HEAD_DIM
DIFF_VROWS = DIFF_VDIM + 16
CONV_WIDTH = 3
RMS_EPS = 1e-6
NEG_INF = -1e30
LOG2E = math.log2(math.e)

LANES = 128
VMEM_LIMIT_BYTES = 56 * 1024 * 1024

TM = 512
SWA_TQ = 512
DIFF_TQ = 512
DIFF_TK = 256
FF_CHUNKS = (768, 768, 768, 512)

NT_DIMS = (((1,), (1,)), ((), ()))


def _params(flags=None):
    return pltpu.CompilerParams(vmem_limit_bytes=VMEM_LIMIT_BYTES, flags=flags)


def _layer_block(arr, l):
    nd = arr.ndim - 1
    return pl.BlockSpec((None,) + arr.shape[1:], lambda *_: (l,) + (0,) * nd)


def _mod_block(mod, l):
    return pl.BlockSpec((None, 1) + mod.shape[2:], lambda bi, i: (l, bi, 0, 0))


def _alibi_slopes(n):
    return [2.0 ** (-8.0 * (i + 1) / n) for i in range(n)]


def _ada_kernel(c_ref, w_ref, b_ref, o_ref):
    c = c_ref[...]
    cs = c * jax.nn.sigmoid(c)
    o_ref[0] = jnp.dot(cs, w_ref[0], preferred_element_type=F32,
                       precision=lax.Precision.HIGHEST) + b_ref[0]


def _ada(c8, w_ada, b_ada):
    depth, d, six_d = w_ada.shape
    tn = 1024
    return pl.pallas_call(
        _ada_kernel,
        grid=(depth, six_d // tn),
        in_specs=[
            pl.BlockSpec((8, d), lambda l, j: (0, 0)),
            pl.BlockSpec((1, d, tn), lambda l, j: (l, 0, j)),
            pl.BlockSpec((1, 1, tn), lambda l, j: (l, 0, j)),
        ],
        out_specs=pl.BlockSpec((1, 8, tn), lambda l, j: (l, 0, j)),
        out_shape=jax.ShapeDtypeStruct((depth, 8, six_d), F32),
        compiler_params=_params(),
        name="ada_mod",
    )(c8, w_ada, b_ada.reshape(depth, 1, six_d))


IN_GROUPS = (("qa", 512), ("kax", 512), ("vax", 256), ("qd", 512), ("kd", 512), ("vd", 512), ("g", 2048))


def _in_proj_kernel(x_ref, mod_ref, g_ref, w_ref, qa_ref, kax_ref, vax_ref, qd_ref, kd_ref, vd_ref, sg_ref):
    x = x_ref[0]
    ms = jnp.mean(x * x, axis=-1, keepdims=True)
    y = x * lax.rsqrt(ms + RMS_EPS) * g_ref[...]
    h = y * (1.0 + mod_ref[0, 1:2, :]) + mod_ref[0, 0:1, :]
    hb = h.astype(BF16)
    outs = dict(qa=qa_ref, kax=kax_ref, vax=vax_ref, qd=qd_ref, kd=kd_ref, vd=vd_ref, g=sg_ref)
    c0 = 0
    for name, width in IN_GROUPS:
        p = jnp.dot(hb, w_ref[:, c0:c0 + width], preferred_element_type=F32)
        if name == "qa":
            outs[name][0] = (p * (HEAD_DIM ** -0.5)).astype(BF16)
        elif name == "qd":
            outs[name][0] = (p * (HEAD_DIM ** -0.5 * LOG2E)).T.astype(BF16)
        elif name == "vd":
            pt = p.T
            ones = jnp.ones((DIFF_VROWS - DIFF_VDIM, DIFF_TK), BF16)
            for hd in range(DIFF_HEADS):
                for t in range(TM // DIFF_TK):
                    outs[name][0, hd, t, :DIFF_VDIM] = pt[hd * DIFF_VDIM:(hd + 1) * DIFF_VDIM,
                                                          t * DIFF_TK:(t + 1) * DIFF_TK].astype(BF16)
                    outs[name][0, hd, t, DIFF_VDIM:] = ones
        elif name == "g":
            outs[name][0] = jax.nn.sigmoid(p).astype(BF16)
        else:
            outs[name][0] = p.astype(BF16)
        c0 += width


def _in_proj(x, mod, norm_g, w_in_x, l):
    b, s, d = x.shape
    row = lambda bi, i: (bi, i, 0)
    out_specs, out_shape = [], []
    for name, w in IN_GROUPS:
        if name == "qd":
            out_specs.append(pl.BlockSpec((1, w, TM), lambda bi, i: (bi, 0, i)))
            out_shape.append(jax.ShapeDtypeStruct((b, w, s), BF16))
        elif name == "vd":
            out_specs.append(pl.BlockSpec((1, DIFF_HEADS, TM // DIFF_TK, DIFF_VROWS, DIFF_TK),
                                          lambda bi, i: (bi, 0, i, 0, 0)))
            out_shape.append(jax.ShapeDtypeStruct((b, DIFF_HEADS, s // DIFF_TK, DIFF_VROWS, DIFF_TK), BF16))
        else:
            out_specs.append(pl.BlockSpec((1, TM, w), row))
            out_shape.append(jax.ShapeDtypeStruct((b, s, w), BF16))
    return pl.pallas_call(
        _in_proj_kernel,
        grid=(b, s // TM),
        in_specs=[
            pl.BlockSpec((1, TM, d), row),
            _mod_block(mod, l),
            _layer_block(norm_g, l),
            _layer_block(w_in_x, l),
        ],
        out_specs=out_specs,
        out_shape=out_shape,
        compiler_params=_params(),
        name="in_proj",
    )(x, mod, norm_g, w_in_x)


def _swa_kernel(sink_ref, q_ref, kc_ref, kp_ref, vc_ref, vp_ref, bias_ref, o_ref, kcat, vcat, *, layer):
    first = pl.program_id(1) == 0
    kcat[0:BLOCK] = kp_ref[0]
    kcat[BLOCK:] = kc_ref[0]
    vcat[0:BLOCK] = vp_ref[0]
    vcat[BLOCK:] = vc_ref[0]
    col = lax.broadcasted_iota(jnp.int32, (BLOCK, 2 * BLOCK), 1)
    lane = lax.broadcasted_iota(jnp.int32, (BLOCK, LANES), 1)
    for sb in range(SWA_TQ // BLOCK):
        r0 = sb * BLOCK
        for t in range(SWA_HEADS // 2):
            kh = t // (SWA_GROUP // 2)
            q_t = q_ref[0, r0:r0 + BLOCK, t * LANES:(t + 1) * LANES]
            v = vcat[r0:r0 + 2 * BLOCK, kh * LANES:(kh + 1) * LANES]
            halves = []
            for e in range(2):
                hq = 2 * t + e
                kx = kcat[r0:r0 + 2 * BLOCK, (2 * kh + e) * LANES:(2 * kh + e + 1) * LANES]
                s = lax.dot_general(q_t, kx, NT_DIMS, preferred_element_type=F32)
                bias = bias_ref[hq]
                valid = bias > 0.5 * NEG_INF
                if sb == 0:
                    valid = valid & jnp.logical_not(first & (col < BLOCK))
                s = jnp.where(valid, s + bias, NEG_INF)
                sink = sink_ref[layer, hq]
                m = jnp.maximum(jnp.max(s, axis=-1, keepdims=True), sink)
                p = jnp.exp(s - m)
                denom = jnp.sum(p, axis=-1, keepdims=True) + jnp.exp(sink - m)
                o = jnp.dot(p.astype(BF16), v, preferred_element_type=F32)
                halves.append(o / denom)
            o_t = jnp.where(lane < HEAD_DIM, halves[0], halves[1])
            o_ref[0, r0:r0 + BLOCK, t * LANES:(t + 1) * LANES] = o_t.astype(BF16)


def _swa_bias_table():
    slopes = _alibi_slopes(SWA_HEADS)
    qi = np.arange(BLOCK)[:, None]
    kj = np.arange(2 * BLOCK)[None, :]
    delta = qi + BLOCK - kj
    valid = (delta >= 0) & (delta < WINDOW)
    tab = np.stack([np.where(valid, -np.float32(sl) * delta.astype(np.float32), np.float32(NEG_INF))
                    for sl in slopes]).astype(np.float32)
    return jnp.asarray(tab)


def _swa(qa, kax, vax, sinks, bias, l):
    b, s, _ = qa.shape
    blocks_per_step = SWA_TQ // BLOCK
    prev_map = lambda bi, i: (bi, jnp.maximum(i * blocks_per_step - 1, 0), 0)
    cur_map = lambda bi, i: (bi, i, 0)
    return pl.pallas_call(
        functools.partial(_swa_kernel, layer=l),
        grid=(b, s // SWA_TQ),
        in_specs=[
            pl.BlockSpec(memory_space=pltpu.SMEM),
            pl.BlockSpec((1, SWA_TQ, 512), cur_map),
            pl.BlockSpec((1, SWA_TQ, 512), cur_map),
            pl.BlockSpec((1, BLOCK, 512), prev_map),
            pl.BlockSpec((1, SWA_TQ, 256), cur_map),
            pl.BlockSpec((1, BLOCK, 256), prev_map),
            pl.BlockSpec((SWA_HEADS, BLOCK, 2 * BLOCK), lambda bi, i: (0, 0, 0)),
        ],
        out_specs=pl.BlockSpec((1, SWA_TQ, 512), cur_map),
        out_shape=jax.ShapeDtypeStruct((b, s, 512), BF16),
        scratch_shapes=[pltpu.VMEM((SWA_TQ + BLOCK, 512), BF16), pltpu.VMEM((SWA_TQ + BLOCK, 256), BF16)],
        compiler_params=_params(),
        name="swa_attn",
    )(sinks, qa, kax, kax, vax, vax, bias)


def _diff_kernel(sc_ref, q_ref, k_ref, v_ref, bias_ref, lam_ref, g_ref, o_ref, *scratch, layer):
    lam_init = sc_ref[layer, DIFF_HEADS]
    lp = lam_ref[...]
    lam = (jnp.exp(jnp.sum(lp[0:1] * lp[1:2], axis=-1, keepdims=True))
           - jnp.exp(jnp.sum(lp[2:3] * lp[3:4], axis=-1, keepdims=True)) + lam_init)
    per_set = len(scratch) // 2
    for hd in range(DIFF_HEADS):
        _diff_head(hd, lam, sc_ref[layer, hd] * LOG2E, lam_init, q_ref, k_ref, v_ref, bias_ref, g_ref, o_ref,
                   *scratch[(hd % 2) * per_set:(hd % 2 + 1) * per_set])


def _diff_head(hd, lam, slope2, lam_init, q_ref, k_ref, v_ref, bias_ref, g_ref, o_ref,
               sa1, sa2, sb1, sb2, pa1, pa2, pb1, pb2, acc1, acc2):
    i = pl.program_id(1)
    hcols = slice(hd * 2 * HEAD_DIM, (hd + 1) * 2 * HEAD_DIM)
    qt = q_ref[0, hcols, :]
    qrow = lax.broadcasted_iota(jnp.int32, qt.shape, 0)
    zero = jnp.zeros_like(qt)
    q1 = jnp.where(qrow < HEAD_DIM, qt, zero)
    q2 = jnp.where(qrow < HEAD_DIM, zero, qt)
    tiles_per_q = DIFF_TQ // DIFF_TK
    first_diag = tiles_per_q * i
    n_tiles = first_diag + tiles_per_q

    acc1[...] = jnp.zeros_like(acc1)
    acc2[...] = jnp.zeros_like(acc2)

    def stage_a(t, s1, s2):
        k0 = pl.multiple_of(t * DIFF_TK, DIFF_TK)
        k = k_ref[0, pl.ds(k0, DIFF_TK), hcols]
        bias = bias_ref[hd, jnp.clip(t - first_diag + 1, 0, tiles_per_q)]
        x1 = jnp.dot(k, q1, preferred_element_type=F32) + bias
        x2 = jnp.dot(k, q2, preferred_element_type=F32) + bias
        s1[...] = x1
        s2[...] = x2
        return jnp.max(x1, axis=0, keepdims=True), jnp.max(x2, axis=0, keepdims=True)

    def stage_b(t, mt, s, p, m_old):
        cj = -slope2 * ((first_diag - t) * DIFF_TK).astype(F32)
        m_new = jnp.maximum(m_old, mt + cj)
        p[...] = jnp.exp2(s[...] - (m_new - cj)).astype(BF16)
        return m_new, jnp.exp2(m_old - m_new)

    def stage_c(t, p1, p2, alpha1, alpha2):
        vt = v_ref[0, hd, t]
        acc1[...] = alpha1 * acc1[...] + jnp.dot(vt, p1[...], preferred_element_type=F32)
        acc2[...] = alpha2 * acc2[...] + jnp.dot(vt, p2[...], preferred_element_type=F32)

    m0 = jnp.full((1, DIFF_TQ), NEG_INF, F32)
    mta1, mta2 = stage_a(0, sa1, sa2)
    mtb1, mtb2 = stage_a(1, sb1, sb2)
    m1, ala1 = stage_b(0, mta1, sa1, pa1, m0)
    m2, ala2 = stage_b(0, mta2, sa2, pa2, m0)

    def body(g, carry):
        m1, m2, ala1, ala2, mtb1, mtb2 = carry
        t = 2 * g
        stage_c(t - 2, pa1, pa2, ala1, ala2)
        m1, alb1 = stage_b(t - 1, mtb1, sb1, pb1, m1)
        m2, alb2 = stage_b(t - 1, mtb2, sb2, pb2, m2)
        mta1, mta2 = stage_a(t, sa1, sa2)
        stage_c(t - 1, pb1, pb2, alb1, alb2)
        m1, ala1 = stage_b(t, mta1, sa1, pa1, m1)
        m2, ala2 = stage_b(t, mta2, sa2, pa2, m2)
        mtb1, mtb2 = stage_a(t + 1, sb1, sb2)
        return m1, m2, ala1, ala2, mtb1, mtb2

    carry = lax.fori_loop(1, n_tiles // 2, body, (m1, m2, ala1, ala2, mtb1, mtb2))
    m1, m2, ala1, ala2, mtb1, mtb2 = carry
    stage_c(n_tiles - 2, pa1, pa2, ala1, ala2)
    _, alb1 = stage_b(n_tiles - 1, mtb1, sb1, pb1, m1)
    _, alb2 = stage_b(n_tiles - 1, mtb2, sb2, pb2, m2)
    stage_c(n_tiles - 1, pb1, pb2, alb1, alb2)

    l1 = acc1[DIFF_VDIM:DIFF_VDIM + 1, :]
    l2 = acc2[DIFF_VDIM:DIFF_VDIM + 1, :]
    ot = acc1[:DIFF_VDIM, :] / l1 - lam * (acc2[:DIFF_VDIM, :] / l2)
    ms = jnp.mean(ot * ot, axis=0, keepdims=True)
    yt = ot * lax.rsqrt(ms + RMS_EPS)
    o_ref[0, :, hcols] = (yt.T * g_ref[...] * (1.0 - lam_init)).astype(BF16)


def _diff_bias_table():
    slopes = _alibi_slopes(DIFF_HEADS)
    kr = np.arange(DIFF_TK)[:, None]
    qc = np.arange(DIFF_TQ)[None, :]
    delta = (qc - kr).astype(np.float64)
    heads = []
    for sl in slopes:
        base = -sl * LOG2E * delta
        variants = [base] + [np.where(kr + d * DIFF_TK <= qc, base, NEG_INF)
                             for d in range(DIFF_TQ // DIFF_TK)]
        heads.append(np.stack(variants))
    return jnp.asarray(np.stack(heads).astype(np.float32))


def _diff(qdt, kd, vdt, scalars, bias, lam_p, subln_g, l):
    b, _, s = qdt.shape
    return pl.pallas_call(
        functools.partial(_diff_kernel, layer=l),
        grid=(b, s // DIFF_TQ),
        in_specs=[
            pl.BlockSpec(memory_space=pltpu.SMEM),
            pl.BlockSpec((1, DIFF_HEADS * 2 * HEAD_DIM, DIFF_TQ), lambda bi, i: (bi, 0, i)),
            pl.BlockSpec((1, s, DIFF_HEADS * 2 * HEAD_DIM), lambda bi, i: (bi, 0, 0)),
            pl.BlockSpec((1,) + vdt.shape[1:], lambda bi, i: (bi, 0, 0, 0, 0)),
            pl.BlockSpec(bias.shape, lambda bi, i: (0, 0, 0, 0)),
            _layer_block(lam_p, l),
            _layer_block(subln_g, l),
        ],
        out_specs=pl.BlockSpec((1, DIFF_TQ, DIFF_HEADS * DIFF_VDIM), lambda bi, i: (bi, i, 0)),
        out_shape=jax.ShapeDtypeStruct((b, s, DIFF_HEADS * DIFF_VDIM), BF16),
        scratch_shapes=([pltpu.VMEM((DIFF_TK, DIFF_TQ), F32)] * 4 + [pltpu.VMEM((DIFF_TK, DIFF_TQ), BF16)] * 4
                        + [pltpu.VMEM((DIFF_VROWS, DIFF_TQ), F32)] * 2) * 2,
        compiler_params=_params(),
        name="diff_attn",
    )(scalars, qdt, kd, vdt, bias, lam_p, subln_g)


def _merge_kernel(oa_ref, od_ref, sg_ref, x_ref, mod_ref, wbs_ref, wbd_ref, wo_ref, o_ref):
    d = x_ref.shape[-1]
    pa = jnp.dot(oa_ref[0], wbs_ref[...], preferred_element_type=F32)
    pd = jnp.dot(od_ref[0], wbd_ref[...], preferred_element_type=F32)
    merged = sg_ref[0, :, :d].astype(F32) * pa + sg_ref[0, :, d:].astype(F32) * pd
    y = jnp.dot(merged.astype(BF16), wo_ref[...], preferred_element_type=F32)
    o_ref[0] = x_ref[0] + mod_ref[0, 2:3, :] * y


def _merge(out_a, out_d, sg, x, mod, w_bs, w_bd, w_o, l):
    b, s, d = x.shape
    row = lambda bi, i: (bi, i, 0)
    return pl.pallas_call(
        _merge_kernel,
        grid=(b, s // TM),
        in_specs=[
            pl.BlockSpec((1, TM, out_a.shape[-1]), row),
            pl.BlockSpec((1, TM, out_d.shape[-1]), row),
            pl.BlockSpec((1, TM, 2 * d), row),
            pl.BlockSpec((1, TM, d), row),
            _mod_block(mod, l),
            _layer_block(w_bs, l),
            _layer_block(w_bd, l),
            _layer_block(w_o, l),
        ],
        out_specs=pl.BlockSpec((1, TM, d), row),
        out_shape=jax.ShapeDtypeStruct((b, s, d), F32),
        compiler_params=_params(),
        name="merge_out",
    )(out_a, out_d, sg, x, mod, w_bs, w_bd, w_o)


def _ffn_kernel(x_ref, mod_ref, g_ref, wu_ref, cw_ref, cb_ref, wd_ref, fg_ref, o_ref, act, carry, *, final):
    d_ff = wd_ref.shape[0]
    tm = x_ref.shape[1]

    @pl.when(pl.program_id(1) == 0)
    def _():
        carry[...] = jnp.zeros_like(carry)

    x = x_ref[0]
    ms = jnp.mean(x * x, axis=-1, keepdims=True)
    y = x * lax.rsqrt(ms + RMS_EPS) * g_ref[...]
    h = y * (1.0 + mod_ref[0, 4:5, :]) + mod_ref[0, 3:4, :]
    hb = h.astype(BF16)

    c0 = 0
    for width in FF_CHUNKS:
        a = jnp.dot(hb, wu_ref[:, c0:c0 + width], preferred_element_type=F32)
        gate = jnp.dot(hb, wu_ref[:, d_ff + c0:d_ff + c0 + width], preferred_element_type=F32)
        full = jnp.concatenate([carry[:, c0:c0 + width], a], axis=0)
        carry[:, c0:c0 + width] = a[tm - 8:, :]
        a1 = pltpu.roll(full, 1, 0)[8:]
        a2 = pltpu.roll(full, 2, 0)[8:]
        z = (a2 * cw_ref[0:1, c0:c0 + width] + a1 * cw_ref[1:2, c0:c0 + width]
             + a * cw_ref[2:3, c0:c0 + width] + cb_ref[:, c0:c0 + width])
        gelu = 0.5 * z * (1.0 + lax.erf(z * (2.0 ** -0.5)))
        act[:, c0:c0 + width] = (gelu * gate).astype(BF16)
        c0 += width

    yd = jnp.dot(act[...], wd_ref[...], preferred_element_type=F32)
    out = x + mod_ref[0, 5:6, :] * yd
    if final:
        ms2 = jnp.mean(out * out, axis=-1, keepdims=True)
        out = out * lax.rsqrt(ms2 + RMS_EPS) * fg_ref[...]
    o_ref[0] = out


def _ffn(x, mod, norm_g, w_up, conv_w, conv_b, w_down, final_g, l, final):
    b, s, d = x.shape
    d_ff = w_down.shape[1]
    row = lambda bi, i: (bi, i, 0)
    return pl.pallas_call(
        functools.partial(_ffn_kernel, final=final),
        grid=(b, s // TM),
        in_specs=[
            pl.BlockSpec((1, TM, d), row),
            _mod_block(mod, l),
            _layer_block(norm_g, l),
            _layer_block(w_up, l),
            _layer_block(conv_w, l),
            _layer_block(conv_b, l),
            _layer_block(w_down, l),
            pl.BlockSpec((1, d), lambda bi, i: (0, 0)),
        ],
        out_specs=pl.BlockSpec((1, TM, d), row),
        out_shape=jax.ShapeDtypeStruct((b, s, d), F32),
        scratch_shapes=[pltpu.VMEM((TM, d_ff), BF16), pltpu.VMEM((8, d_ff), F32)],
        compiler_params=_params(),
        name="conv_ffn",
    )(x, mod, norm_g, w_up, conv_w, conv_b, w_down, final_g.reshape(1, d))


def _expand_in_weight(w):
    qa, ka, va, rest = w[..., :512], w[..., 512:640], w[..., 640:768], w[..., 768:]
    z = jnp.zeros(w.shape[:-1] + (HEAD_DIM,), w.dtype)
    kax, vax = [], []
    for kh in range(SWA_KV_HEADS):
        k = ka[..., kh * HEAD_DIM:(kh + 1) * HEAD_DIM]
        v = va[..., kh * HEAD_DIM:(kh + 1) * HEAD_DIM]
        kax += [k, z, z, k]
        vax += [v, v]
    return jnp.concatenate([qa] + kax + vax + [rest], axis=-1).astype(BF16)


def kernel(x, c, w_ada, b_ada, norm1_g, w_in, swa_sinks, diff_lambda, diff_subln_g, w_branch_swa,
           w_branch_diff, w_out, norm2_g, w_up, conv_w, conv_b, w_down, final_g):
    b, s, d = x.shape
    depth = w_ada.shape[0]

    c8 = jnp.zeros((8, d), F32).at[:b].set(c)
    mod = _ada(c8, w_ada, b_ada).reshape(depth, 8, 6, d)

    swa_bias = _swa_bias_table()
    diff_bias = _diff_bias_table()
    lam_inits = [0.8 - 0.6 * math.exp(-0.3 * l) for l in range(depth)]
    diff_scalars = jnp.asarray([_alibi_slopes(DIFF_HEADS) + [li] for li in lam_inits], F32)

    w_in_x = _expand_in_weight(w_in)
    w_bs, w_bd, w_o = w_branch_swa.astype(BF16), w_branch_diff.astype(BF16), w_out.astype(BF16)
    w_up_b, w_down_b = w_up.astype(BF16), w_down.astype(BF16)
    n1g, n2g = norm1_g.reshape(depth, 1, d), norm2_g.reshape(depth, 1, d)
    subln_g = diff_subln_g.reshape(depth, 1, DIFF_VDIM)
    conv_b3 = conv_b.reshape(depth, 1, -1)

    for l in range(depth):
        qa, kax, vax, qd, kd, vd, sg = _in_proj(x, mod, n1g, w_in_x, l)
        out_a = _swa(qa, kax, vax, swa_sinks, swa_bias, l)
        out_d = _diff(qd, kd, vd, diff_scalars, diff_bias, diff_lambda, subln_g, l)
        x = _merge(out_a, out_d, sg, x, mod, w_bs, w_bd, w_o, l)
        x = _ffn(x, mod, n2g, w_up_b, conv_w, conv_b3, w_down_b, final_g, l, final=(l == depth - 1))
    return x
```

```python
import functools
import math

import jax
import jax.numpy as jnp
import numpy as np
from jax import lax
from jax.experimental import pallas as pl
from jax.experimental.pallas import tpu as pltpu

F32 = jnp.float32
BF16 = jnp.bfloat16

HEAD_DIM = 64
SWA_HEADS = 8
SWA_KV_HEADS = 2
SWA_GROUP = SWA_HEADS // SWA_KV_HEADS
WINDOW = 128
BLOCK = 128
DIFF_HEADS = 4
DIFF_VDIM = 2 * HEAD_DIM
DIFF_VROWS = DIFF_VDIM + 16
CONV_WIDTH = 3
RMS_EPS = 1e-6
NEG_INF = -1e30
LOG2E = math.log2(math.e)

LANES = 128
VMEM_LIMIT_BYTES = 56 * 1024 * 1024

TM = 512
SWA_TQ = 512
DIFF_TQ = 512
DIFF_TK = 256
FF_CHUNKS = (768, 768, 768, 512)

NT_DIMS = (((1,), (1,)), ((), ()))


def _params(flags=None):
    return pltpu.CompilerParams(vmem_limit_bytes=VMEM_LIMIT_BYTES, flags=flags)


def _layer_block(arr, l):
    nd = arr.ndim - 1
    return pl.BlockSpec((None,) + arr.shape[1:], lambda *_: (l,) + (0,) * nd, pipeline_mode=pl.Buffered(1))


def _mod_block(mod, l):
    return pl.BlockSpec((None, 1) + mod.shape[2:], lambda bi, i: (l, bi, 0, 0))


def _alibi_slopes(n):
    return [2.0 ** (-8.0 * (i + 1) / n) for i in range(n)]


def _ada_kernel(c_ref, w_ref, b_ref, o_ref):
    c = c_ref[...]
    cs = c * jax.nn.sigmoid(c)
    o_ref[0] = jnp.dot(cs.astype(BF16), w_ref[0].astype(BF16), preferred_element_type=F32) + b_ref[0]


def _ada(c8, w_ada, b_ada):
    depth, d, six_d = w_ada.shape
    tn = 1024
    return pl.pallas_call(
        _ada_kernel,
        grid=(depth, six_d // tn),
        in_specs=[
            pl.BlockSpec((8, d), lambda l, j: (0, 0)),
            pl.BlockSpec((1, d, tn), lambda l, j: (l, 0, j)),
            pl.BlockSpec((1, 1, tn), lambda l, j: (l, 0, j)),
        ],
        out_specs=pl.BlockSpec((1, 8, tn), lambda l, j: (l, 0, j)),
        out_shape=jax.ShapeDtypeStruct((depth, 8, six_d), F32),
        compiler_params=_params(),
        name="ada_mod",
    )(c8, w_ada, b_ada.reshape(depth, 1, six_d))


IN_GROUPS = (("qa", 512), ("kva", 256), ("qd", 512), ("kd", 512), ("vd", 512), ("g", 2048))


def _in_proj_kernel(x_ref, mod_ref, g_ref, w_ref, qa_ref, kva_ref, qd_ref, kd_ref, vd_ref, sg_ref):
    x = x_ref[0]
    ms = jnp.mean(x * x, axis=-1, keepdims=True)
    y = x * lax.rsqrt(ms + RMS_EPS) * g_ref[...]
    h = y * (1.0 + mod_ref[0, 1:2, :]) + mod_ref[0, 0:1, :]
    hb = h.astype(BF16)
    outs = dict(qa=qa_ref, kva=kva_ref, qd=qd_ref, kd=kd_ref, vd=vd_ref, g=sg_ref)
    c0 = 0
    for name, width in IN_GROUPS:
        p = jnp.dot(hb, w_ref[:, c0:c0 + width], preferred_element_type=F32)
        if name == "qa":
            outs[name][0] = (p * (HEAD_DIM ** -0.5)).astype(BF16)
        elif name == "qd":
            outs[name][0] = (p * (HEAD_DIM ** -0.5 * LOG2E)).T.astype(BF16)
        elif name == "vd":
            pt = p.T
            ones = jnp.ones((DIFF_VROWS - DIFF_VDIM, DIFF_TK), BF16)
            for hd in range(DIFF_HEADS):
                for t in range(TM // DIFF_TK):
                    outs[name][0, hd, t, :DIFF_VDIM] = pt[hd * DIFF_VDIM:(hd + 1) * DIFF_VDIM,
                                                          t * DIFF_TK:(t + 1) * DIFF_TK].astype(BF16)
                    outs[name][0, hd, t, DIFF_VDIM:] = ones
        elif name == "g":
            outs[name][0] = jax.nn.sigmoid(p).astype(BF16)
        else:
            outs[name][0] = p.astype(BF16)
        c0 += width


def _in_proj(x, mod, norm_g, w_in, l):
    b, s, d = x.shape
    row = lambda bi, i: (bi, i, 0)
    out_specs, out_shape = [], []
    for name, w in IN_GROUPS:
        if name == "qd":
            out_specs.append(pl.BlockSpec((1, w, TM), lambda bi, i: (bi, 0, i)))
            out_shape.append(jax.ShapeDtypeStruct((b, w, s), BF16))
        elif name == "vd":
            out_specs.append(pl.BlockSpec((1, DIFF_HEADS, TM // DIFF_TK, DIFF_VROWS, DIFF_TK),
                                          lambda bi, i: (bi, 0, i, 0, 0)))
            out_shape.append(jax.ShapeDtypeStruct((b, DIFF_HEADS, s // DIFF_TK, DIFF_VROWS, DIFF_TK), BF16))
        else:
            out_specs.append(pl.BlockSpec((1, TM, w), row))
            out_shape.append(jax.ShapeDtypeStruct((b, s, w), BF16))
    return pl.pallas_call(
        _in_proj_kernel,
        grid=(b, s // TM),
        in_specs=[
            pl.BlockSpec((1, TM, d), row),
            _mod_block(mod, l),
            _layer_block(norm_g, l),
            _layer_block(w_in, l),
        ],
        out_specs=out_specs,
        out_shape=out_shape,
        compiler_params=_params(),
        name="in_proj",
    )(x, mod, norm_g, w_in)


def _swa_kernel(sink_ref, q_ref, kc_ref, kp_ref, vc_ref, vp_ref, bias_ref, o_ref, kcat, vcat, *, layer):
    first = pl.program_id(1) == 0
    pr = lax.broadcasted_iota(jnp.int32, (LANES, LANES), 0)
    pc = lax.broadcasted_iota(jnp.int32, (LANES, LANES), 1)
    swap = jnp.where(pc == (pr + HEAD_DIM) % LANES, 1.0, 0.0).astype(BF16)
    low = lax.broadcasted_iota(jnp.int32, (SWA_TQ + BLOCK, LANES), 1) < HEAD_DIM
    k = jnp.concatenate([kp_ref[0], kc_ref[0]], axis=0)
    v = jnp.concatenate([vp_ref[0], vc_ref[0]], axis=0)
    ks = jnp.dot(k, swap, preferred_element_type=F32).astype(BF16)
    vs = jnp.dot(v, swap, preferred_element_type=F32).astype(BF16)
    zero = jnp.zeros_like(k)
    kcat[:, 0 * LANES:1 * LANES] = jnp.where(low, k, zero)
    kcat[:, 1 * LANES:2 * LANES] = jnp.where(low, zero, ks)
    kcat[:, 2 * LANES:3 * LANES] = jnp.where(low, ks, zero)
    kcat[:, 3 * LANES:4 * LANES] = jnp.where(low, zero, k)
    vcat[:, 0 * LANES:1 * LANES] = jnp.where(low, v, vs)
    vcat[:, 1 * LANES:2 * LANES] = jnp.where(low, vs, v)
    col = lax.broadcasted_iota(jnp.int32, (BLOCK, 2 * BLOCK), 1)
    lane = lax.broadcasted_iota(jnp.int32, (BLOCK, LANES), 1)
    for sb in range(SWA_TQ // BLOCK):
        r0 = sb * BLOCK
        for t in range(SWA_HEADS // 2):
            kh = t // (SWA_GROUP // 2)
            q_t = q_ref[0, r0:r0 + BLOCK, t * LANES:(t + 1) * LANES]
            v = vcat[r0:r0 + 2 * BLOCK, kh * LANES:(kh + 1) * LANES]
            halves = []
            for e in range(2):
                hq = 2 * t + e
                kx = kcat[r0:r0 + 2 * BLOCK, (2 * kh + e) * LANES:(2 * kh + e + 1) * LANES]
                s = lax.dot_general(q_t, kx, NT_DIMS, preferred_element_type=F32)
                bias = bias_ref[hq]
                valid = bias > 0.5 * NEG_INF
                if sb == 0:
                    valid = valid & jnp.logical_not(first & (col < BLOCK))
                s = jnp.where(valid, s + bias, NEG_INF)
                sink = sink_ref[layer, hq]
                m = jnp.maximum(jnp.max(s, axis=-1, keepdims=True), sink)
                p = jnp.exp(s - m)
                denom = jnp.sum(p, axis=-1, keepdims=True) + jnp.exp(sink - m)
                o = jnp.dot(p.astype(BF16), v, preferred_element_type=F32)
                halves.append(o / denom)
            o_t = jnp.where(lane < HEAD_DIM, halves[0], halves[1])
            o_ref[0, r0:r0 + BLOCK, t * LANES:(t + 1) * LANES] = o_t.astype(BF16)


def _swa_bias_table():
    slopes = _alibi_slopes(SWA_HEADS)
    qi = np.arange(BLOCK)[:, None]
    kj = np.arange(2 * BLOCK)[None, :]
    delta = qi + BLOCK - kj
    valid = (delta >= 0) & (delta < WINDOW)
    tab = np.stack([np.where(valid, -np.float32(sl) * delta.astype(np.float32), np.float32(NEG_INF))
                    for sl in slopes]).astype(np.float32)
    return jnp.asarray(tab)


def _swa(qa, kva, sinks, bias, l):
    b, s, _ = qa.shape
    blocks_per_step = SWA_TQ // BLOCK
    cur_map = lambda bi, i: (bi, i, 0)
    prev = lambda bi, i: jnp.maximum(i * blocks_per_step - 1, 0)
    return pl.pallas_call(
        functools.partial(_swa_kernel, layer=l),
        grid=(b, s // SWA_TQ),
        in_specs=[
            pl.BlockSpec(memory_space=pltpu.SMEM),
            pl.BlockSpec((1, SWA_TQ, 512), cur_map),
            pl.BlockSpec((1, SWA_TQ, LANES), lambda bi, i: (bi, i, 0)),
            pl.BlockSpec((1, BLOCK, LANES), lambda bi, i: (bi, prev(bi, i), 0)),
            pl.BlockSpec((1, SWA_TQ, LANES), lambda bi, i: (bi, i, 1)),
            pl.BlockSpec((1, BLOCK, LANES), lambda bi, i: (bi, prev(bi, i), 1)),
            pl.BlockSpec((SWA_HEADS, BLOCK, 2 * BLOCK), lambda bi, i: (0, 0, 0)),
        ],
        out_specs=pl.BlockSpec((1, SWA_TQ, 512), cur_map),
        out_shape=jax.ShapeDtypeStruct((b, s, 512), BF16),
        scratch_shapes=[pltpu.VMEM((SWA_TQ + BLOCK, 512), BF16), pltpu.VMEM((SWA_TQ + BLOCK, 256), BF16)],
        compiler_params=_params(),
        name="swa_attn",
    )(sinks, qa, kva, kva, kva, kva, bias)


def _diff_kernel(sc_ref, q_ref, k_ref, v_ref, bias_ref, lam_ref, g_ref, o_ref, *scratch, layer):
    lam_init = sc_ref[layer, DIFF_HEADS]
    lp = lam_ref[...]
    lam = (jnp.exp(jnp.sum(lp[0:1] * lp[1:2], axis=-1, keepdims=True))
           - jnp.exp(jnp.sum(lp[2:3] * lp[3:4], axis=-1, keepdims=True)) + lam_init)
    per_set = len(scratch) // 2
    for hd in range(DIFF_HEADS):
        _diff_head(hd, lam, sc_ref[layer, hd] * LOG2E, lam_init, q_ref, k_ref, v_ref, bias_ref, g_ref, o_ref,
                   *scratch[(hd % 2) * per_set:(hd % 2 + 1) * per_set])


def _diff_head(hd, lam, slope2, lam_init, q_ref, k_ref, v_ref, bias_ref, g_ref, o_ref,
               sa1, sa2, sb1, sb2, acc1, acc2):
    i = pl.program_id(1)
    hcols = slice(hd * 2 * HEAD_DIM, (hd + 1) * 2 * HEAD_DIM)
    qt = q_ref[0, hcols, :]
    qrow = lax.broadcasted_iota(jnp.int32, qt.shape, 0)
    zero = jnp.zeros_like(qt)
    q1 = jnp.where(qrow < HEAD_DIM, qt, zero)
    q2 = jnp.where(qrow < HEAD_DIM, zero, qt)
    tiles_per_q = DIFF_TQ // DIFF_TK
    first_diag = tiles_per_q * i
    n_tiles = first_diag + tiles_per_q

    acc1[...] = jnp.zeros_like(acc1)
    acc2[...] = jnp.zeros_like(acc2)

    def stage_a(t, s1, s2):
        k0 = pl.multiple_of(t * DIFF_TK, DIFF_TK)
        k = k_ref[0, pl.ds(k0, DIFF_TK), hcols]
        bias = bias_ref[hd, jnp.clip(t - first_diag + 1, 0, tiles_per_q)]
        x1 = jnp.dot(k, q1, preferred_element_type=F32) + bias
        x2 = jnp.dot(k, q2, preferred_element_type=F32) + bias
        s1[...] = x1
        s2[...] = x2
        return jnp.max(x1, axis=0, keepdims=True), jnp.max(x2, axis=0, keepdims=True)

    def stage_b(t, mt, s, acc, m_old):
        cj = -slope2 * ((first_diag - t) * DIFF_TK).astype(F32)
        m_new = jnp.maximum(m_old, mt + cj)
        p = jnp.exp2(s[...] - (m_new - cj)).astype(BF16)
        vt = v_ref[0, hd, t]
        acc[...] = jnp.exp2(m_old - m_new) * acc[...] + jnp.dot(vt, p, preferred_element_type=F32)
        return m_new

    m0 = jnp.full((1, DIFF_TQ), NEG_INF, F32)
    mta1, mta2 = stage_a(0, sa1, sa2)
    mtb1, mtb2 = stage_a(1, sb1, sb2)
    m1 = stage_b(0, mta1, sa1, acc1, m0)
    m2 = stage_b(0, mta2, sa2, acc2, m0)

    def body(g, carry):
        m1, m2, mtb1, mtb2 = carry
        t = 2 * g
        mta1, mta2 = stage_a(t, sa1, sa2)
        m1 = stage_b(t - 1, mtb1, sb1, acc1, m1)
        m2 = stage_b(t - 1, mtb2, sb2, acc2, m2)
        mtb1, mtb2 = stage_a(t + 1, sb1, sb2)
        m1 = stage_b(t, mta1, sa1, acc1, m1)
        m2 = stage_b(t, mta2, sa2, acc2, m2)
        return m1, m2, mtb1, mtb2

    m1, m2, mtb1, mtb2 = lax.fori_loop(1, n_tiles // 2, body, (m1, m2, mtb1, mtb2))
    stage_b(n_tiles - 1, mtb1, sb1, acc1, m1)
    stage_b(n_tiles - 1, mtb2, sb2, acc2, m2)

    l1 = acc1[DIFF_VDIM:DIFF_VDIM + 1, :]
    l2 = acc2[DIFF_VDIM:DIFF_VDIM + 1, :]
    ot = acc1[:DIFF_VDIM, :] / l1 - lam * (acc2[:DIFF_VDIM, :] / l2)
    ms = jnp.mean(ot * ot, axis=0, keepdims=True)
    yt = ot * lax.rsqrt(ms + RMS_EPS)
    o_ref[0, :, hcols] = (yt.T * g_ref[...] * (1.0 - lam_init)).astype(BF16)


def _diff_bias_table():
    slopes = _alibi_slopes(DIFF_HEADS)
    kr = np.arange(DIFF_TK)[:, None]
    qc = np.arange(DIFF_TQ)[None, :]
    delta = (qc - kr).astype(np.float64)
    heads = []
    for sl in slopes:
        base = -sl * LOG2E * delta
        variants = [base] + [np.where(kr + d * DIFF_TK <= qc, base, NEG_INF)
                             for d in range(DIFF_TQ // DIFF_TK)]
        heads.append(np.stack(variants))
    return jnp.asarray(np.stack(heads).astype(np.float32))


def _diff(qdt, kd, vdt, scalars, bias, lam_p, subln_g, l):
    b, _, s = qdt.shape
    return pl.pallas_call(
        functools.partial(_diff_kernel, layer=l),
        grid=(b, s // DIFF_TQ),
        in_specs=[
            pl.BlockSpec(memory_space=pltpu.SMEM),
            pl.BlockSpec((1, DIFF_HEADS * 2 * HEAD_DIM, DIFF_TQ), lambda bi, i: (bi, 0, i)),
            pl.BlockSpec((1, s, DIFF_HEADS * 2 * HEAD_DIM), lambda bi, i: (bi, 0, 0)),
            pl.BlockSpec((1,) + vdt.shape[1:], lambda bi, i: (bi, 0, 0, 0, 0)),
            pl.BlockSpec(bias.shape, lambda bi, i: (0, 0, 0, 0)),
            _layer_block(lam_p, l),
            _layer_block(subln_g, l),
        ],
        out_specs=pl.BlockSpec((1, DIFF_TQ, DIFF_HEADS * DIFF_VDIM), lambda bi, i: (bi, i, 0)),
        out_shape=jax.ShapeDtypeStruct((b, s, DIFF_HEADS * DIFF_VDIM), BF16),
        scratch_shapes=([pltpu.VMEM((DIFF_TK, DIFF_TQ), F32)] * 4
                        + [pltpu.VMEM((DIFF_VROWS, DIFF_TQ), F32)] * 2) * 2,
        compiler_params=_params(),
        name="diff_attn",
    )(scalars, qdt, kd, vdt, bias, lam_p, subln_g)


def _mix_ffn_kernel(oa_ref, od_ref, sg_ref, x_ref, mod_ref, wbs_ref, wbd_ref, wo_ref, g_ref, wu_ref, cw_ref,
                    cb_ref, wd_ref, fg_ref, o_ref, act, carry, *, final):
    d = x_ref.shape[-1]
    d_ff = wd_ref.shape[0]
    tm = x_ref.shape[1]

    @pl.when(pl.program_id(1) == 0)
    def _():
        carry[...] = jnp.zeros_like(carry)

    pa = jnp.dot(oa_ref[0], wbs_ref[...], preferred_element_type=F32)
    pd = jnp.dot(od_ref[0], wbd_ref[...], preferred_element_type=F32)
    merged = sg_ref[0, :, :d].astype(F32) * pa + sg_ref[0, :, d:].astype(F32) * pd
    x = x_ref[0] + mod_ref[0, 2:3, :] * jnp.dot(merged.astype(BF16), wo_ref[...], preferred_element_type=F32)

    ms = jnp.mean(x * x, axis=-1, keepdims=True)
    y = x * lax.rsqrt(ms + RMS_EPS) * g_ref[...]
    h = y * (1.0 + mod_ref[0, 4:5, :]) + mod_ref[0, 3:4, :]
    hb = h.astype(BF16)

    c0 = 0
    for width in FF_CHUNKS:
        a = jnp.dot(hb, wu_ref[:, c0:c0 + width], preferred_element_type=F32)
        gate = jnp.dot(hb, wu_ref[:, d_ff + c0:d_ff + c0 + width], preferred_element_type=F32)
        full = jnp.concatenate([carry[:, c0:c0 + width], a], axis=0)
        carry[:, c0:c0 + width] = a[tm - 8:, :]
        a1 = pltpu.roll(full, 1, 0)[8:]
        a2 = pltpu.roll(full, 2, 0)[8:]
        z = (a2 * cw_ref[0:1, c0:c0 + width] + a1 * cw_ref[1:2, c0:c0 + width]
             + a * cw_ref[2:3, c0:c0 + width] + cb_ref[:, c0:c0 + width])
        gelu = 0.5 * z * (1.0 + lax.erf(z * (2.0 ** -0.5)))
        act[:, c0:c0 + width] = (gelu * gate).astype(BF16)
        c0 += width

    yd = jnp.dot(act[...], wd_ref[...], preferred_element_type=F32)
    out = x + mod_ref[0, 5:6, :] * yd
    if final:
        ms2 = jnp.mean(out * out, axis=-1, keepdims=True)
        out = out * lax.rsqrt(ms2 + RMS_EPS) * fg_ref[...]
    o_ref[0] = out


def _mix_ffn(out_a, out_d, sg, x, mod, w_bs, w_bd, w_o, norm_g, w_up, conv_w, conv_b, w_down, final_g, l, final):
    b, s, d = x.shape
    d_ff = w_down.shape[1]
    row = lambda bi, i: (bi, i, 0)
    return pl.pallas_call(
        functools.partial(_mix_ffn_kernel, final=final),
        grid=(b, s // TM),
        in_specs=[
            pl.BlockSpec((1, TM, out_a.shape[-1]), row),
            pl.BlockSpec((1, TM, out_d.shape[-1]), row),
            pl.BlockSpec((1, TM, 2 * d), row),
            pl.BlockSpec((1, TM, d), row),
            _mod_block(mod, l),
            _layer_block(w_bs, l),
            _layer_block(w_bd, l),
            _layer_block(w_o, l),
            _layer_block(norm_g, l),
            _layer_block(w_up, l),
            _layer_block(conv_w, l),
            _layer_block(conv_b, l),
            _layer_block(w_down, l),
            pl.BlockSpec((1, d), lambda bi, i: (0, 0)),
        ],
        out_specs=pl.BlockSpec((1, TM, d), row),
        out_shape=jax.ShapeDtypeStruct((b, s, d), F32),
        scratch_shapes=[pltpu.VMEM((TM, d_ff), BF16), pltpu.VMEM((8, d_ff), F32)],
        compiler_params=_params(),
        name="mix_ffn",
    )(out_a, out_d, sg, x, mod, w_bs, w_bd, w_o, norm_g, w_up, conv_w, conv_b, w_down, final_g.reshape(1, d))


def kernel(x, c, w_ada, b_ada, norm1_g, w_in, swa_sinks, diff_lambda, diff_subln_g, w_branch_swa,
           w_branch_diff, w_out, norm2_g, w_up, conv_w, conv_b, w_down, final_g):
    b, s, d = x.shape
    depth = w_ada.shape[0]

    c8 = jnp.zeros((8, d), F32).at[:b].set(c)
    mod = _ada(c8, w_ada, b_ada).reshape(depth, 8, 6, d)

    swa_bias = _swa_bias_table()
    diff_bias = _diff_bias_table()
    lam_inits = [0.8 - 0.6 * math.exp(-0.3 * l) for l in range(depth)]
    diff_scalars = jnp.asarray([_alibi_slopes(DIFF_HEADS) + [li] for li in lam_inits], F32)

    w_in_b = w_in.astype(BF16)
    w_bs, w_bd, w_o = w_branch_swa.astype(BF16), w_branch_diff.astype(BF16), w_out.astype(BF16)
    w_up_b, w_down_b = w_up.astype(BF16), w_down.astype(BF16)
    n1g, n2g = norm1_g.reshape(depth, 1, d), norm2_g.reshape(depth, 1, d)
    subln_g = diff_subln_g.reshape(depth, 1, DIFF_VDIM)
    conv_b3 = conv_b.reshape(depth, 1, -1)

    for l in range(depth):
        qa, kva, qd, kd, vd, sg = _in_proj(x, mod, n1g, w_in_b, l)
        out_a = _swa(qa, kva, swa_sinks, swa_bias, l)
        out_d = _diff(qd, kd, vd, diff_scalars, diff_bias, diff_lambda, subln_g, l)
        x = _mix_ffn(out_a, out_d, sg, x, mod, w_bs, w_bd, w_o, n2g, w_up_b, conv_w, conv_b3, w_down_b, final_g,
                     l, final=(l == depth - 1))
    return x
```

```python
import functools
import math

import jax
import jax.numpy as jnp
import numpy as np
from jax import lax
from jax.experimental import pallas as pl
from jax.experimental.pallas import tpu as pltpu

F32 = jnp.float32
BF16 = jnp.bfloat16

HEAD_DIM = 64
SWA_HEADS = 8
SWA_KV_HEADS = 2
SWA_GROUP = SWA_HEADS // SWA_KV_HEADS
WINDOW = 128
BLOCK = 128
DIFF_HEADS = 4
DIFF_VDIM = 2 * HEAD_DIM
DIFF_VROWS = DIFF_VDIM + 16
CONV_WIDTH = 3
RMS_EPS = 1e-6
NEG_INF = -1e30
LOG2E = math.log2(math.e)

LANES = 128
VMEM_LIMIT_BYTES = 56 * 1024 * 1024

TM = 512
SWA_TQ = 512
DIFF_TQ = 512
DIFF_TK = 256
FF_CHUNKS = (768, 768, 768, 512)

NT_DIMS = (((1,), (1,)), ((), ()))


def _params(flags=None):
    return pltpu.CompilerParams(vmem_limit_bytes=VMEM_LIMIT_BYTES, flags=flags)


def _layer_block(arr, l):
    nd = arr.ndim - 1
    return pl.BlockSpec((None,) + arr.shape[1:], lambda *_: (l,) + (0,) * nd, pipeline_mode=pl.Buffered(1))


def _mod_block(mod, l):
    return pl.BlockSpec((None, 1) + mod.shape[2:], lambda bi, i: (l, bi, 0, 0))


def _alibi_slopes(n):
    return [2.0 ** (-8.0 * (i + 1) / n) for i in range(n)]


def _ada_kernel(c_ref, w_ref, b_ref, o_ref):
    c = c_ref[...]
    cs = c * jax.nn.sigmoid(c)
    o_ref[0] = jnp.dot(cs.astype(BF16), w_ref[0].astype(BF16), preferred_element_type=F32) + b_ref[0]


def _ada(c8, w_ada, b_ada):
    depth, d, six_d = w_ada.shape
    tn = 1024
    return pl.pallas_call(
        _ada_kernel,
        grid=(depth, six_d // tn),
        in_specs=[
            pl.BlockSpec((8, d), lambda l, j: (0, 0)),
            pl.BlockSpec((1, d, tn), lambda l, j: (l, 0, j)),
            pl.BlockSpec((1, 1, tn), lambda l, j: (l, 0, j)),
        ],
        out_specs=pl.BlockSpec((1, 8, tn), lambda l, j: (l, 0, j)),
        out_shape=jax.ShapeDtypeStruct((depth, 8, six_d), F32),
        compiler_params=_params(),
        name="ada_mod",
    )(c8, w_ada, b_ada.reshape(depth, 1, six_d))


IN_GROUPS = (("qa", 512), ("kva", 256), ("qd", 512), ("kd", 512), ("vd", 512), ("g", 2048))
IN_OFFSETS = {name: sum(w for _, w in IN_GROUPS[:j]) for j, (name, _) in enumerate(IN_GROUPS)}
IN_ORDER = ("g", "vd", "qd", "kd", "qa", "kva")
DIFF_FEATS = 3


def _in_proj_kernel(x_ref, mod_ref, g_ref, w_ref, feat_ref, qa_ref, kva_ref, qd_ref, kd1_ref, kd2_ref, vd_ref,
                    sg_ref):
    x = x_ref[0]
    ms = jnp.mean(x * x, axis=-1, keepdims=True)
    y = x * lax.rsqrt(ms + RMS_EPS) * g_ref[...]
    h = y * (1.0 + mod_ref[0, 1:2, :]) + mod_ref[0, 0:1, :]
    hb = h.astype(BF16)
    widths = dict(IN_GROUPS)
    for name in IN_ORDER:
        c0 = IN_OFFSETS[name]
        p = jnp.dot(hb, w_ref[:, c0:c0 + widths[name]], preferred_element_type=F32)
        if name == "qa":
            qa_ref[0] = (p * (HEAD_DIM ** -0.5)).astype(BF16)
        elif name == "kva":
            kva_ref[0] = p.astype(BF16)
        elif name == "qd":
            qd_ref[0] = (p * (HEAD_DIM ** -0.5 * LOG2E)).T.astype(BF16)
        elif name == "kd":
            lane = lax.broadcasted_iota(jnp.int32, p.shape, 1)
            first_half = (lane & (2 * HEAD_DIM - 1)) < HEAD_DIM
            kd1_ref[0] = jnp.where(first_half, p, feat_ref[0]).astype(BF16)
            kd2_ref[0] = jnp.where(first_half, feat_ref[1], p).astype(BF16)
        elif name == "vd":
            pt = p.T
            ones = jnp.ones((DIFF_VROWS - DIFF_VDIM, DIFF_TK), BF16)
            for hd in range(DIFF_HEADS):
                for t in range(TM // DIFF_TK):
                    vd_ref[0, hd, t, :DIFF_VDIM] = pt[hd * DIFF_VDIM:(hd + 1) * DIFF_VDIM,
                                                      t * DIFF_TK:(t + 1) * DIFF_TK].astype(BF16)
                    vd_ref[0, hd, t, DIFF_VDIM:] = ones
        else:
            sg_ref[0] = jax.nn.sigmoid(p).astype(BF16)


def _key_feature_table():
    r = (np.arange(TM) % DIFF_TK).astype(np.float32)[:, None]
    lane = np.arange(DIFF_HEADS * 2 * HEAD_DIM)[None, :] % (2 * HEAD_DIM)
    f1 = np.where((lane >= HEAD_DIM) & (lane < HEAD_DIM + DIFF_FEATS), r, 0.0)
    f2 = np.where(lane < DIFF_FEATS, r, 0.0)
    return jnp.asarray(np.stack([f1, f2]).astype(np.float32))


def _in_proj(x, mod, norm_g, w_in, feat, l):
    b, s, d = x.shape
    row = lambda bi, i: (bi, i, 0)
    dq = DIFF_HEADS * 2 * HEAD_DIM
    plain = lambda w: (pl.BlockSpec((1, TM, w), row), jax.ShapeDtypeStruct((b, s, w), BF16))
    outs = [
        plain(512),
        plain(256),
        (pl.BlockSpec((1, dq, TM), lambda bi, i: (bi, 0, i)),
         jax.ShapeDtypeStruct((b, dq, s), BF16)),
        plain(dq),
        plain(dq),
        (pl.BlockSpec((1, DIFF_HEADS, TM // DIFF_TK, DIFF_VROWS, DIFF_TK),
                      lambda bi, i: (bi, 0, i, 0, 0)),
         jax.ShapeDtypeStruct((b, DIFF_HEADS, s // DIFF_TK, DIFF_VROWS, DIFF_TK), BF16)),
        plain(2 * d),
    ]
    return pl.pallas_call(
        _in_proj_kernel,
        grid=(b, s // TM),
        in_specs=[
            pl.BlockSpec((1, TM, d), row),
            _mod_block(mod, l),
            _layer_block(norm_g, l),
            _layer_block(w_in, l),
            pl.BlockSpec(feat.shape, lambda bi, i: (0, 0, 0), pipeline_mode=pl.Buffered(1)),
        ],
        out_specs=[o[0] for o in outs],
        out_shape=[o[1] for o in outs],
        compiler_params=_params(),
        name="in_proj",
    )(x, mod, norm_g, w_in, feat)


def _swa_kernel(sink_ref, q_ref, kc_ref, kp_ref, vc_ref, vp_ref, bias_ref, o_ref, kcat, vcat, *, layer):
    first = pl.program_id(1) == 0
    pr = lax.broadcasted_iota(jnp.int32, (LANES, LANES), 0)
    pc = lax.broadcasted_iota(jnp.int32, (LANES, LANES), 1)
    swap = jnp.where(pc == (pr + HEAD_DIM) % LANES, 1.0, 0.0).astype(BF16)
    low = lax.broadcasted_iota(jnp.int32, (SWA_TQ + BLOCK, LANES), 1) < HEAD_DIM
    k = jnp.concatenate([kp_ref[0], kc_ref[0]], axis=0)
    v = jnp.concatenate([vp_ref[0], vc_ref[0]], axis=0)
    ks = jnp.dot(k, swap, preferred_element_type=F32).astype(BF16)
    vs = jnp.dot(v, swap, preferred_element_type=F32).astype(BF16)
    zero = jnp.zeros_like(k)
    kcat[:, 0 * LANES:1 * LANES] = jnp.where(low, k, zero)
    kcat[:, 1 * LANES:2 * LANES] = jnp.where(low, zero, ks)
    kcat[:, 2 * LANES:3 * LANES] = jnp.where(low, ks, zero)
    kcat[:, 3 * LANES:4 * LANES] = jnp.where(low, zero, k)
    vcat[:, 0 * LANES:1 * LANES] = jnp.where(low, v, vs)
    vcat[:, 1 * LANES:2 * LANES] = jnp.where(low, vs, v)
    col = lax.broadcasted_iota(jnp.int32, (BLOCK, 2 * BLOCK), 1)
    lane = lax.broadcasted_iota(jnp.int32, (BLOCK, LANES), 1)
    for sb in range(SWA_TQ // BLOCK):
        r0 = sb * BLOCK
        for t in range(SWA_HEADS // 2):
            kh = t // (SWA_GROUP // 2)
            q_t = q_ref[0, r0:r0 + BLOCK, t * LANES:(t + 1) * LANES]
            v = vcat[r0:r0 + 2 * BLOCK, kh * LANES:(kh + 1) * LANES]
            halves = []
            for e in range(2):
                hq = 2 * t + e
                kx = kcat[r0:r0 + 2 * BLOCK, (2 * kh + e) * LANES:(2 * kh + e + 1) * LANES]
                s = lax.dot_general(q_t, kx, NT_DIMS, preferred_element_type=F32)
                bias = bias_ref[hq]
                valid = bias > 0.5 * NEG_INF
                if sb == 0:
                    valid = valid & jnp.logical_not(first & (col < BLOCK))
                s = jnp.where(valid, s + bias, NEG_INF)
                sink = sink_ref[layer, hq]
                m = jnp.maximum(jnp.max(s, axis=-1, keepdims=True), sink)
                p = jnp.exp(s - m)
                denom = jnp.sum(p, axis=-1, keepdims=True) + jnp.exp(sink - m)
                o = jnp.dot(p.astype(BF16), v, preferred_element_type=F32)
                halves.append(o / denom)
            o_t = jnp.where(lane < HEAD_DIM, halves[0], halves[1])
            o_ref[0, r0:r0 + BLOCK, t * LANES:(t + 1) * LANES] = o_t.astype(BF16)


def _swa_bias_table():
    slopes = _alibi_slopes(SWA_HEADS)
    qi = np.arange(BLOCK)[:, None]
    kj = np.arange(2 * BLOCK)[None, :]
    delta = qi + BLOCK - kj
    valid = (delta >= 0) & (delta < WINDOW)
    tab = np.stack([np.where(valid, -np.float32(sl) * delta.astype(np.float32), np.float32(NEG_INF))
                    for sl in slopes]).astype(np.float32)
    return jnp.asarray(tab)


def _swa(qa, kva, sinks, bias, l):
    b, s, _ = qa.shape
    blocks_per_step = SWA_TQ // BLOCK
    cur_map = lambda bi, i: (bi, i, 0)
    prev = lambda bi, i: jnp.maximum(i * blocks_per_step - 1, 0)
    return pl.pallas_call(
        functools.partial(_swa_kernel, layer=l),
        grid=(b, s // SWA_TQ),
        in_specs=[
            pl.BlockSpec(memory_space=pltpu.SMEM),
            pl.BlockSpec((1, SWA_TQ, 512), cur_map),
            pl.BlockSpec((1, SWA_TQ, LANES), lambda bi, i: (bi, i, 0)),
            pl.BlockSpec((1, BLOCK, LANES), lambda bi, i: (bi, prev(bi, i), 0)),
            pl.BlockSpec((1, SWA_TQ, LANES), lambda bi, i: (bi, i, 1)),
            pl.BlockSpec((1, BLOCK, LANES), lambda bi, i: (bi, prev(bi, i), 1)),
            pl.BlockSpec((SWA_HEADS, BLOCK, 2 * BLOCK), lambda bi, i: (0, 0, 0)),
        ],
        out_specs=pl.BlockSpec((1, SWA_TQ, 512), cur_map),
        out_shape=jax.ShapeDtypeStruct((b, s, 512), BF16),
        scratch_shapes=[pltpu.VMEM((SWA_TQ + BLOCK, 512), BF16), pltpu.VMEM((SWA_TQ + BLOCK, 256), BF16)],
        compiler_params=_params(),
        name="swa_attn",
    )(sinks, qa, kva, kva, kva, kva, bias)


def _diff_kernel(sc_ref, q_ref, k1_ref, k2_ref, v_ref, slope_ref, mask_ref, lam_ref, g_ref, o_ref, *scratch, layer):
    lam_init = sc_ref[layer, DIFF_HEADS]
    lp = lam_ref[...]
    lam = (jnp.exp(jnp.sum(lp[0:1] * lp[1:2], axis=-1, keepdims=True))
           - jnp.exp(jnp.sum(lp[2:3] * lp[3:4], axis=-1, keepdims=True)) + lam_init)
    per_set = len(scratch) // 2
    for hd in range(DIFF_HEADS):
        _diff_head(hd, lam, sc_ref[layer, hd] * LOG2E, lam_init, q_ref, k1_ref, k2_ref, v_ref, slope_ref, mask_ref,
                   g_ref, o_ref, *scratch[(hd % 2) * per_set:(hd % 2 + 1) * per_set])


def _diff_head(hd, lam, slope2, lam_init, q_ref, k1_ref, k2_ref, v_ref, slope_ref, mask_ref, g_ref, o_ref,
               sa1, sa2, sb1, sb2, acc1, acc2):
    i = pl.program_id(1)
    hcols = slice(hd * 2 * HEAD_DIM, (hd + 1) * 2 * HEAD_DIM)
    qt = q_ref[0, hcols, :]
    qrow = lax.broadcasted_iota(jnp.int32, qt.shape, 0)
    q1 = jnp.where(qrow < HEAD_DIM, qt, slope_ref[hd, 0])
    q2 = jnp.where(qrow < HEAD_DIM, slope_ref[hd, 1], qt)
    tiles_per_q = DIFF_TQ // DIFF_TK
    first_diag = tiles_per_q * i
    n_tiles = first_diag + tiles_per_q

    acc1[...] = jnp.zeros_like(acc1)
    acc2[...] = jnp.zeros_like(acc2)

    def tile_at(n):
        return jnp.where(n < tiles_per_q, first_diag + n, n_tiles - 1 - n)

    def stage_a(t, s1, s2, diag=None):
        k0 = pl.multiple_of(t * DIFF_TK, DIFF_TK)
        x1 = jnp.dot(k1_ref[0, pl.ds(k0, DIFF_TK), hcols], q1, preferred_element_type=F32)
        x2 = jnp.dot(k2_ref[0, pl.ds(k0, DIFF_TK), hcols], q2, preferred_element_type=F32)
        if diag is not None:
            x1 = x1 + mask_ref[diag]
            x2 = x2 + mask_ref[diag]
        s1[...] = x1
        s2[...] = x2
        return jnp.max(x1, axis=0, keepdims=True), jnp.max(x2, axis=0, keepdims=True)

    def stage_b(t, mt, s, acc, m_old):
        cj = -slope2 * ((first_diag - t) * DIFF_TK).astype(F32)
        m_new = jnp.maximum(m_old, mt + cj)
        p = jnp.exp2(s[...] - (m_new - cj)).astype(BF16)
        vt = v_ref[0, hd, t]
        acc[...] = jnp.exp2(m_old - m_new) * acc[...] + jnp.dot(vt, p, preferred_element_type=F32)
        return m_new

    m0 = jnp.full((1, DIFF_TQ), NEG_INF, F32)
    mta1, mta2 = stage_a(first_diag, sa1, sa2, diag=0)
    mtb1, mtb2 = stage_a(first_diag + 1, sb1, sb2, diag=1)
    m1 = stage_b(first_diag, mta1, sa1, acc1, m0)
    m2 = stage_b(first_diag, mta2, sa2, acc2, m0)

    def body(g, carry):
        m1, m2, mtb1, mtb2 = carry
        n = 2 * g
        mta1, mta2 = stage_a(tile_at(n), sa1, sa2)
        m1 = stage_b(tile_at(n - 1), mtb1, sb1, acc1, m1)
        m2 = stage_b(tile_at(n - 1), mtb2, sb2, acc2, m2)
        mtb1, mtb2 = stage_a(tile_at(n + 1), sb1, sb2)
        m1 = stage_b(tile_at(n), mta1, sa1, acc1, m1)
        m2 = stage_b(tile_at(n), mta2, sa2, acc2, m2)
        return m1, m2, mtb1, mtb2

    m1, m2, mtb1, mtb2 = lax.fori_loop(1, n_tiles // 2, body, (m1, m2, mtb1, mtb2))
    stage_b(tile_at(n_tiles - 1), mtb1, sb1, acc1, m1)
    stage_b(tile_at(n_tiles - 1), mtb2, sb2, acc2, m2)

    l1 = acc1[DIFF_VDIM:DIFF_VDIM + 1, :]
    l2 = acc2[DIFF_VDIM:DIFF_VDIM + 1, :]
    ot = acc1[:DIFF_VDIM, :] / l1 - lam * (acc2[:DIFF_VDIM, :] / l2)
    ms = jnp.mean(ot * ot, axis=0, keepdims=True)
    yt = ot * lax.rsqrt(ms + RMS_EPS)
    o_ref[0, :, hcols] = (yt.T * g_ref[...] * (1.0 - lam_init)).astype(BF16)


def _diff_slope_table():
    tab = np.zeros((DIFF_HEADS, 2, 2 * HEAD_DIM, DIFF_TQ), np.float32)
    for hd, sl in enumerate(_alibi_slopes(DIFF_HEADS)):
        rest = np.float32(sl * LOG2E)
        for j in range(DIFF_FEATS):
            piece = np.float32(rest.astype(jnp.bfloat16))
            tab[hd, 0, HEAD_DIM + j, :] = piece
            tab[hd, 1, j, :] = piece
            rest = np.float32(rest - piece)
    return jnp.asarray(tab.astype(jnp.bfloat16))


def _diff_mask_table():
    kr = np.arange(DIFF_TK)[:, None]
    qc = np.arange(DIFF_TQ)[None, :]
    return jnp.asarray(np.stack([np.where(kr + d * DIFF_TK <= qc, 0.0, NEG_INF)
                                 for d in range(DIFF_TQ // DIFF_TK)]).astype(np.float32))


def _diff(qdt, kd1, kd2, vdt, scalars, slopes, mask, lam_p, subln_g, l):
    b, _, s = qdt.shape
    const = lambda arr: pl.BlockSpec(arr.shape, lambda bi, i: (0,) * arr.ndim, pipeline_mode=pl.Buffered(1))
    kspec = pl.BlockSpec((1, s, DIFF_HEADS * 2 * HEAD_DIM), lambda bi, i: (bi, 0, 0))
    return pl.pallas_call(
        functools.partial(_diff_kernel, layer=l),
        grid=(b, s // DIFF_TQ),
        in_specs=[
            pl.BlockSpec(memory_space=pltpu.SMEM),
            pl.BlockSpec((1, DIFF_HEADS * 2 * HEAD_DIM, DIFF_TQ), lambda bi, i: (bi, 0, i)),
            kspec,
            kspec,
            pl.BlockSpec((1,) + vdt.shape[1:], lambda bi, i: (bi, 0, 0, 0, 0)),
            const(slopes),
            const(mask),
            _layer_block(lam_p, l),
            _layer_block(subln_g, l),
        ],
        out_specs=pl.BlockSpec((1, DIFF_TQ, DIFF_HEADS * DIFF_VDIM), lambda bi, i: (bi, i, 0)),
        out_shape=jax.ShapeDtypeStruct((b, s, DIFF_HEADS * DIFF_VDIM), BF16),
        scratch_shapes=([pltpu.VMEM((DIFF_TK, DIFF_TQ), F32)] * 4
                        + [pltpu.VMEM((DIFF_VROWS, DIFF_TQ), F32)] * 2) * 2,
        compiler_params=_params(),
        name="diff_attn",
    )(scalars, qdt, kd1, kd2, vdt, slopes, mask, lam_p, subln_g)


def _mix_ffn_kernel(oa_ref, od_ref, sg_ref, x_ref, mod_ref, wbs_ref, wbd_ref, wo_ref, g_ref, wu_ref, cw_ref,
                    cb_ref, wd_ref, fg_ref, o_ref, act, carry, *, final):
    d = x_ref.shape[-1]
    d_ff = wd_ref.shape[0]
    tm = x_ref.shape[1]

    @pl.when(pl.program_id(1) == 0)
    def _():
        carry[...] = jnp.zeros_like(carry)

    pa = jnp.dot(oa_ref[0], wbs_ref[...], preferred_element_type=F32)
    pd = jnp.dot(od_ref[0], wbd_ref[...], preferred_element_type=F32)
    merged = sg_ref[0, :, :d].astype(F32) * pa + sg_ref[0, :, d:].astype(F32) * pd
    x = x_ref[0] + mod_ref[0, 2:3, :] * jnp.dot(merged.astype(BF16), wo_ref[...], preferred_element_type=F32)

    ms = jnp.mean(x * x, axis=-1, keepdims=True)
    y = x * lax.rsqrt(ms + RMS_EPS) * g_ref[...]
    h = y * (1.0 + mod_ref[0, 4:5, :]) + mod_ref[0, 3:4, :]
    hb = h.astype(BF16)

    c0 = 0
    for width in FF_CHUNKS:
        a = jnp.dot(hb, wu_ref[:, c0:c0 + width], preferred_element_type=F32)
        gate = jnp.dot(hb, wu_ref[:, d_ff + c0:d_ff + c0 + width], preferred_element_type=F32)
        full = jnp.concatenate([carry[:, c0:c0 + width], a], axis=0)
        carry[:, c0:c0 + width] = a[tm - 8:, :]
        a1 = pltpu.roll(full, 1, 0)[8:]
        a2 = pltpu.roll(full, 2, 0)[8:]
        z = (a2 * cw_ref[0:1, c0:c0 + width] + a1 * cw_ref[1:2, c0:c0 + width]
             + a * cw_ref[2:3, c0:c0 + width] + cb_ref[:, c0:c0 + width])
        gelu = 0.5 * z * (1.0 + lax.erf(z * (2.0 ** -0.5)))
        act[:, c0:c0 + width] = (gelu * gate).astype(BF16)
        c0 += width

    yd = jnp.dot(act[...], wd_ref[...], preferred_element_type=F32)
    out = x + mod_ref[0, 5:6, :] * yd
    if final:
        ms2 = jnp.mean(out * out, axis=-1, keepdims=True)
        out = out * lax.rsqrt(ms2 + RMS_EPS) * fg_ref[...]
    o_ref[0] = out


def _mix_ffn(out_a, out_d, sg, x, mod, w_bs, w_bd, w_o, norm_g, w_up, conv_w, conv_b, w_down, final_g, l, final):
    b, s, d = x.shape
    d_ff = w_down.shape[1]
    row = lambda bi, i: (bi, i, 0)
    return pl.pallas_call(
        functools.partial(_mix_ffn_kernel, final=final),
        grid=(b, s // TM),
        in_specs=[
            pl.BlockSpec((1, TM, out_a.shape[-1]), row),
            pl.BlockSpec((1, TM, out_d.shape[-1]), row),
            pl.BlockSpec((1, TM, 2 * d), row),
            pl.BlockSpec((1, TM, d), row),
            _mod_block(mod, l),
            _layer_block(w_bs, l),
            _layer_block(w_bd, l),
            _layer_block(w_o, l),
            _layer_block(norm_g, l),
            _layer_block(w_up, l),
            _layer_block(conv_w, l),
            _layer_block(conv_b, l),
            _layer_block(w_down, l),
            pl.BlockSpec((1, d), lambda bi, i: (0, 0)),
        ],
        out_specs=pl.BlockSpec((1, TM, d), row),
        out_shape=jax.ShapeDtypeStruct((b, s, d), F32),
        scratch_shapes=[pltpu.VMEM((TM, d_ff), BF16), pltpu.VMEM((8, d_ff), F32)],
        compiler_params=_params(),
        name="mix_ffn",
    )(out_a, out_d, sg, x, mod, w_bs, w_bd, w_o, norm_g, w_up, conv_w, conv_b, w_down, final_g.reshape(1, d))


def kernel(x, c, w_ada, b_ada, norm1_g, w_in, swa_sinks, diff_lambda, diff_subln_g, w_branch_swa,
           w_branch_diff, w_out, norm2_g, w_up, conv_w, conv_b, w_down, final_g):
    b, s, d = x.shape
    depth = w_ada.shape[0]

    c8 = jnp.zeros((8, d), F32).at[:b].set(c)
    mod = _ada(c8, w_ada, b_ada).reshape(depth, 8, 6, d)

    swa_bias = _swa_bias_table()
    diff_slopes = _diff_slope_table()
    diff_mask = _diff_mask_table()
    key_feat = _key_feature_table()
    lam_inits = [0.8 - 0.6 * math.exp(-0.3 * l) for l in range(depth)]
    diff_scalars = jnp.asarray([_alibi_slopes(DIFF_HEADS) + [li] for li in lam_inits], F32)

    w_in_b = w_in.astype(BF16)
    w_bs, w_bd, w_o = w_branch_swa.astype(BF16), w_branch_diff.astype(BF16), w_out.astype(BF16)
    w_up_b, w_down_b = w_up.astype(BF16), w_down.astype(BF16)
    n1g, n2g = norm1_g.reshape(depth, 1, d), norm2_g.reshape(depth, 1, d)
    subln_g = diff_subln_g.reshape(depth, 1, DIFF_VDIM)
    conv_b3 = conv_b.reshape(depth, 1, -1)

    for l in range(depth):
        qa, kva, qd, kd1, kd2, vd, sg = _in_proj(x, mod, n1g, w_in_b, key_feat, l)
        out_a = _swa(qa, kva, swa_sinks, swa_bias, l)
        out_d = _diff(qd, kd1, kd2, vd, diff_scalars, diff_slopes, diff_mask, diff_lambda, subln_g, l)
        x = _mix_ffn(out_a, out_d, sg, x, mod, w_bs, w_bd, w_o, n2g, w_up_b, conv_w, conv_b3, w_down_b, final_g,
                     l, final=(l == depth - 1))
    return x
```

```python
import functools
import math

import jax
import jax.numpy as jnp
import numpy as np
from jax import lax
from jax.experimental import pallas as pl
from jax.experimental.pallas import tpu as pltpu

F32 = jnp.float32
BF16 = jnp.bfloat16

HEAD_DIM = 64
SWA_HEADS = 8
SWA_KV_HEADS = 2
SWA_GROUP = SWA_HEADS // SWA_KV_HEADS
WINDOW = 128
BLOCK = 128
DIFF_HEADS = 4
DIFF_VDIM = 2 * HEAD_DIM
DIFF_VROWS = DIFF_VDIM + 16
CONV_WIDTH = 3
RMS_EPS = 1e-6
NEG_INF = -1e30
LOG2E = math.log2(math.e)

LANES = 128
VMEM_LIMIT_BYTES = 56 * 1024 * 1024

TM = 512
SWA_TQ = 512
DIFF_TQ = 512
DIFF_TK = 256
FF_CHUNKS = (768, 768, 768, 512)

NT_DIMS = (((1,), (1,)), ((), ()))


def _params(flags=None):
    return pltpu.CompilerParams(vmem_limit_bytes=VMEM_LIMIT_BYTES, flags=flags)


def _layer_block(arr, l):
    nd = arr.ndim - 1
    return pl.BlockSpec((None,) + arr.shape[1:], lambda *_: (l,) + (0,) * nd, pipeline_mode=pl.Buffered(1))


def _mod_block(mod, l):
    return pl.BlockSpec((None, 1) + mod.shape[2:], lambda bi, i: (l, bi, 0, 0))


def _alibi_slopes(n):
    return [2.0 ** (-8.0 * (i + 1) / n) for i in range(n)]


def _ada_kernel(c_ref, w_ref, b_ref, o_ref):
    c = c_ref[...]
    cs = c * jax.nn.sigmoid(c)
    o_ref[0] = jnp.dot(cs.astype(BF16), w_ref[0].astype(BF16), preferred_element_type=F32) + b_ref[0]


def _ada(c8, w_ada, b_ada):
    depth, d, six_d = w_ada.shape
    tn = 1024
    return pl.pallas_call(
        _ada_kernel,
        grid=(depth, six_d // tn),
        in_specs=[
            pl.BlockSpec((8, d), lambda l, j: (0, 0)),
            pl.BlockSpec((1, d, tn), lambda l, j: (l, 0, j)),
            pl.BlockSpec((1, 1, tn), lambda l, j: (l, 0, j)),
        ],
        out_specs=pl.BlockSpec((1, 8, tn), lambda l, j: (l, 0, j)),
        out_shape=jax.ShapeDtypeStruct((depth, 8, six_d), F32),
        compiler_params=_params(),
        name="ada_mod",
    )(c8, w_ada, b_ada.reshape(depth, 1, six_d))


IN_GROUPS = (("qa", 512), ("kva", 256), ("qd", 512), ("kd", 512), ("vd", 512), ("g", 2048))
IN_OFFSETS = {name: sum(w for _, w in IN_GROUPS[:j]) for j, (name, _) in enumerate(IN_GROUPS)}
IN_ORDER = ("g", "vd", "qd", "kd", "qa", "kva")
DIFF_FEATS = 3


def _in_proj_kernel(x_ref, mod_ref, g_ref, w_ref, feat_ref, qa_ref, kva_ref, qd_ref, kd1_ref, kd2_ref, vd_ref,
                    sg_ref):
    x = x_ref[0]
    ms = jnp.mean(x * x, axis=-1, keepdims=True)
    y = x * lax.rsqrt(ms + RMS_EPS) * g_ref[...]
    h = y * (1.0 + mod_ref[0, 1:2, :]) + mod_ref[0, 0:1, :]
    hb = h.astype(BF16)
    widths = dict(IN_GROUPS)
    for name in IN_ORDER:
        c0 = IN_OFFSETS[name]
        p = jnp.dot(hb, w_ref[:, c0:c0 + widths[name]], preferred_element_type=F32)
        if name == "qa":
            qa_ref[0] = (p * (HEAD_DIM ** -0.5 * LOG2E)).astype(BF16)
        elif name == "kva":
            kva_ref[0] = p.astype(BF16)
        elif name == "qd":
            qd_ref[0] = (p * (HEAD_DIM ** -0.5 * LOG2E)).T.astype(BF16)
        elif name == "kd":
            lane = lax.broadcasted_iota(jnp.int32, p.shape, 1)
            first_half = (lane & (2 * HEAD_DIM - 1)) < HEAD_DIM
            kd1_ref[0] = jnp.where(first_half, p, feat_ref[0]).astype(BF16)
            kd2_ref[0] = jnp.where(first_half, feat_ref[1], p).astype(BF16)
        elif name == "vd":
            pt = p.T
            ones = jnp.ones((DIFF_VROWS - DIFF_VDIM, DIFF_TK), BF16)
            for hd in range(DIFF_HEADS):
                for t in range(TM // DIFF_TK):
                    vd_ref[0, hd, t, :DIFF_VDIM] = pt[hd * DIFF_VDIM:(hd + 1) * DIFF_VDIM,
                                                      t * DIFF_TK:(t + 1) * DIFF_TK].astype(BF16)
                    vd_ref[0, hd, t, DIFF_VDIM:] = ones
        else:
            sg_ref[0] = jax.nn.sigmoid(p).astype(BF16)


def _key_feature_table():
    r = (np.arange(TM) % DIFF_TK).astype(np.float32)[:, None]
    lane = np.arange(DIFF_HEADS * 2 * HEAD_DIM)[None, :] % (2 * HEAD_DIM)
    f1 = np.where((lane >= HEAD_DIM) & (lane < HEAD_DIM + DIFF_FEATS), r, 0.0)
    f2 = np.where(lane < DIFF_FEATS, r, 0.0)
    return jnp.asarray(np.stack([f1, f2]).astype(np.float32))


def _in_proj(x, mod, norm_g, w_in, feat, l):
    b, s, d = x.shape
    row = lambda bi, i: (bi, i, 0)
    dq = DIFF_HEADS * 2 * HEAD_DIM
    plain = lambda w: (pl.BlockSpec((1, TM, w), row), jax.ShapeDtypeStruct((b, s, w), BF16))
    outs = [
        plain(512),
        plain(256),
        (pl.BlockSpec((1, dq, TM), lambda bi, i: (bi, 0, i)),
         jax.ShapeDtypeStruct((b, dq, s), BF16)),
        plain(dq),
        plain(dq),
        (pl.BlockSpec((1, DIFF_HEADS, TM // DIFF_TK, DIFF_VROWS, DIFF_TK),
                      lambda bi, i: (bi, 0, i, 0, 0)),
         jax.ShapeDtypeStruct((b, DIFF_HEADS, s // DIFF_TK, DIFF_VROWS, DIFF_TK), BF16)),
        plain(2 * d),
    ]
    return pl.pallas_call(
        _in_proj_kernel,
        grid=(b, s // TM),
        in_specs=[
            pl.BlockSpec((1, TM, d), row),
            _mod_block(mod, l),
            _layer_block(norm_g, l),
            _layer_block(w_in, l),
            pl.BlockSpec(feat.shape, lambda bi, i: (0, 0, 0), pipeline_mode=pl.Buffered(1)),
        ],
        out_specs=[o[0] for o in outs],
        out_shape=[o[1] for o in outs],
        compiler_params=_params(),
        name="in_proj",
    )(x, mod, norm_g, w_in, feat)


def _swa_kernel(sink_ref, q_ref, kc_ref, kp_ref, vc_ref, vp_ref, bias_ref, o_ref, kcat, vcat, *, layer):
    first = pl.program_id(1) == 0
    pr = lax.broadcasted_iota(jnp.int32, (LANES, LANES), 0)
    pc = lax.broadcasted_iota(jnp.int32, (LANES, LANES), 1)
    swap = jnp.where(pc == (pr + HEAD_DIM) % LANES, 1.0, 0.0).astype(BF16)
    low = lax.broadcasted_iota(jnp.int32, (SWA_TQ + BLOCK, LANES), 1) < HEAD_DIM
    k = jnp.concatenate([kp_ref[0], kc_ref[0]], axis=0)
    v = jnp.concatenate([vp_ref[0], vc_ref[0]], axis=0)
    ks = jnp.dot(k, swap, preferred_element_type=F32).astype(BF16)
    vs = jnp.dot(v, swap, preferred_element_type=F32).astype(BF16)
    zero = jnp.zeros_like(k)
    kcat[:, 0 * LANES:1 * LANES] = jnp.where(low, k, zero)
    kcat[:, 1 * LANES:2 * LANES] = jnp.where(low, zero, ks)
    kcat[:, 2 * LANES:3 * LANES] = jnp.where(low, ks, zero)
    kcat[:, 3 * LANES:4 * LANES] = jnp.where(low, zero, k)
    vcat[:, 0 * LANES:1 * LANES] = jnp.where(low, v, vs)
    vcat[:, 1 * LANES:2 * LANES] = jnp.where(low, vs, v)
    col = lax.broadcasted_iota(jnp.int32, (BLOCK, 2 * BLOCK), 1)
    lane = lax.broadcasted_iota(jnp.int32, (BLOCK, LANES), 1)
    for sb in range(SWA_TQ // BLOCK):
        r0 = sb * BLOCK
        for t in range(SWA_HEADS // 2):
            kh = t // (SWA_GROUP // 2)
            q_t = q_ref[0, r0:r0 + BLOCK, t * LANES:(t + 1) * LANES]
            v = vcat[r0:r0 + 2 * BLOCK, kh * LANES:(kh + 1) * LANES]
            halves = []
            for e in range(2):
                hq = 2 * t + e
                kx = kcat[r0:r0 + 2 * BLOCK, (2 * kh + e) * LANES:(2 * kh + e + 1) * LANES]
                s = lax.dot_general(q_t, kx, NT_DIMS, preferred_element_type=F32)
                s = s + bias_ref[hq]
                if sb == 0:
                    s = jnp.where(first & (col < BLOCK), NEG_INF, s)
                sink = sink_ref[layer, hq] * LOG2E
                m = jnp.maximum(jnp.max(s, axis=-1, keepdims=True), sink)
                p = jnp.exp2(s - m)
                denom = jnp.sum(p, axis=-1, keepdims=True) + jnp.exp2(sink - m)
                o = jnp.dot(p.astype(BF16), v, preferred_element_type=F32)
                halves.append(o / denom)
            o_t = jnp.where(lane < HEAD_DIM, halves[0], halves[1])
            o_ref[0, r0:r0 + BLOCK, t * LANES:(t + 1) * LANES] = o_t.astype(BF16)


def _swa_bias_table():
    slopes = _alibi_slopes(SWA_HEADS)
    qi = np.arange(BLOCK)[:, None]
    kj = np.arange(2 * BLOCK)[None, :]
    delta = qi + BLOCK - kj
    valid = (delta >= 0) & (delta < WINDOW)
    tab = np.stack([np.where(valid, -sl * LOG2E * delta.astype(np.float64), NEG_INF) for sl in slopes])
    return jnp.asarray(tab.astype(np.float32))


def _swa(qa, kva, sinks, bias, l):
    b, s, _ = qa.shape
    blocks_per_step = SWA_TQ // BLOCK
    cur_map = lambda bi, i: (bi, i, 0)
    prev = lambda bi, i: jnp.maximum(i * blocks_per_step - 1, 0)
    return pl.pallas_call(
        functools.partial(_swa_kernel, layer=l),
        grid=(b, s // SWA_TQ),
        in_specs=[
            pl.BlockSpec(memory_space=pltpu.SMEM),
            pl.BlockSpec((1, SWA_TQ, 512), cur_map),
            pl.BlockSpec((1, SWA_TQ, LANES), lambda bi, i: (bi, i, 0)),
            pl.BlockSpec((1, BLOCK, LANES), lambda bi, i: (bi, prev(bi, i), 0)),
            pl.BlockSpec((1, SWA_TQ, LANES), lambda bi, i: (bi, i, 1)),
            pl.BlockSpec((1, BLOCK, LANES), lambda bi, i: (bi, prev(bi, i), 1)),
            pl.BlockSpec((SWA_HEADS, BLOCK, 2 * BLOCK), lambda bi, i: (0, 0, 0)),
        ],
        out_specs=pl.BlockSpec((1, SWA_TQ, 512), cur_map),
        out_shape=jax.ShapeDtypeStruct((b, s, 512), BF16),
        scratch_shapes=[pltpu.VMEM((SWA_TQ + BLOCK, 512), BF16), pltpu.VMEM((SWA_TQ + BLOCK, 256), BF16)],
        compiler_params=_params(),
        name="swa_attn",
    )(sinks, qa, kva, kva, kva, kva, bias)


def _diff_kernel(sc_ref, q_ref, k1_ref, k2_ref, v_ref, slope_ref, mask_ref, lam_ref, g_ref, o_ref, *scratch, layer):
    lam_init = sc_ref[layer, DIFF_HEADS]
    lp = lam_ref[...]
    lam = (jnp.exp(jnp.sum(lp[0:1] * lp[1:2], axis=-1, keepdims=True))
           - jnp.exp(jnp.sum(lp[2:3] * lp[3:4], axis=-1, keepdims=True)) + lam_init)
    per_set = len(scratch) // 2
    for hd in range(DIFF_HEADS):
        _diff_head(hd, lam, sc_ref[layer, hd] * LOG2E, lam_init, q_ref, k1_ref, k2_ref, v_ref, slope_ref, mask_ref,
                   g_ref, o_ref, *scratch[(hd % 2) * per_set:(hd % 2 + 1) * per_set])


def _diff_head(hd, lam, slope2, lam_init, q_ref, k1_ref, k2_ref, v_ref, slope_ref, mask_ref, g_ref, o_ref,
               sa1, sa2, sb1, sb2, acc1, acc2):
    i = pl.program_id(1)
    hcols = slice(hd * 2 * HEAD_DIM, (hd + 1) * 2 * HEAD_DIM)
    qt = q_ref[0, hcols, :]
    qrow = lax.broadcasted_iota(jnp.int32, qt.shape, 0)
    q1 = jnp.where(qrow < HEAD_DIM, qt, slope_ref[hd, 0])
    q2 = jnp.where(qrow < HEAD_DIM, slope_ref[hd, 1], qt)
    tiles_per_q = DIFF_TQ // DIFF_TK
    first_diag = tiles_per_q * i
    n_tiles = first_diag + tiles_per_q

    acc1[...] = jnp.zeros_like(acc1)
    acc2[...] = jnp.zeros_like(acc2)

    def tile_at(n):
        return jnp.where(n < tiles_per_q, first_diag + n, n_tiles - 1 - n)

    def stage_a(t, s1, s2, diag=None):
        k0 = pl.multiple_of(t * DIFF_TK, DIFF_TK)
        x1 = jnp.dot(k1_ref[0, pl.ds(k0, DIFF_TK), hcols], q1, preferred_element_type=F32)
        x2 = jnp.dot(k2_ref[0, pl.ds(k0, DIFF_TK), hcols], q2, preferred_element_type=F32)
        if diag is not None:
            x1 = x1 + mask_ref[diag]
            x2 = x2 + mask_ref[diag]
        s1[...] = x1
        s2[...] = x2
        return jnp.max(x1, axis=0, keepdims=True), jnp.max(x2, axis=0, keepdims=True)

    def stage_b(t, mt, s, acc, m_old):
        cj = -slope2 * ((first_diag - t) * DIFF_TK).astype(F32)
        m_new = jnp.maximum(m_old, mt + cj)
        p = jnp.exp2(s[...] - (m_new - cj)).astype(BF16)
        vt = v_ref[0, hd, t]
        acc[...] = jnp.exp2(m_old - m_new) * acc[...] + jnp.dot(vt, p, preferred_element_type=F32)
        return m_new

    m0 = jnp.full((1, DIFF_TQ), NEG_INF, F32)
    mta1, mta2 = stage_a(first_diag, sa1, sa2, diag=0)
    mtb1, mtb2 = stage_a(first_diag + 1, sb1, sb2, diag=1)
    m1 = stage_b(first_diag, mta1, sa1, acc1, m0)
    m2 = stage_b(first_diag, mta2, sa2, acc2, m0)

    def body(g, carry):
        m1, m2, mtb1, mtb2 = carry
        n = 2 * g
        mta1, mta2 = stage_a(tile_at(n), sa1, sa2)
        m1 = stage_b(tile_at(n - 1), mtb1, sb1, acc1, m1)
        m2 = stage_b(tile_at(n - 1), mtb2, sb2, acc2, m2)
        mtb1, mtb2 = stage_a(tile_at(n + 1), sb1, sb2)
        m1 = stage_b(tile_at(n), mta1, sa1, acc1, m1)
        m2 = stage_b(tile_at(n), mta2, sa2, acc2, m2)
        return m1, m2, mtb1, mtb2

    n_pairs = n_tiles // 2 - 1
    odd = n_pairs % 2
    carry = lax.fori_loop(1, 1 + odd, body, (m1, m2, mtb1, mtb2))
    m1, m2, mtb1, mtb2 = lax.fori_loop(
        0, n_pairs // 2, lambda j, c: body(2 + odd + 2 * j, body(1 + odd + 2 * j, c)), carry)
    stage_b(tile_at(n_tiles - 1), mtb1, sb1, acc1, m1)
    stage_b(tile_at(n_tiles - 1), mtb2, sb2, acc2, m2)

    l1 = acc1[DIFF_VDIM:DIFF_VDIM + 1, :]
    l2 = acc2[DIFF_VDIM:DIFF_VDIM + 1, :]
    ot = acc1[:DIFF_VDIM, :] / l1 - lam * (acc2[:DIFF_VDIM, :] / l2)
    ms = jnp.mean(ot * ot, axis=0, keepdims=True)
    yt = ot * lax.rsqrt(ms + RMS_EPS)
    o_ref[0, :, hcols] = (yt.T * g_ref[...] * (1.0 - lam_init)).astype(BF16)


def _diff_slope_table():
    tab = np.zeros((DIFF_HEADS, 2, 2 * HEAD_DIM, DIFF_TQ), np.float32)
    for hd, sl in enumerate(_alibi_slopes(DIFF_HEADS)):
        rest = np.float32(sl * LOG2E)
        for j in range(DIFF_FEATS):
            piece = np.float32(rest.astype(jnp.bfloat16))
            tab[hd, 0, HEAD_DIM + j, :] = piece
            tab[hd, 1, j, :] = piece
            rest = np.float32(rest - piece)
    return jnp.asarray(tab.astype(jnp.bfloat16))


def _diff_mask_table():
    kr = np.arange(DIFF_TK)[:, None]
    qc = np.arange(DIFF_TQ)[None, :]
    return jnp.asarray(np.stack([np.where(kr + d * DIFF_TK <= qc, 0.0, NEG_INF)
                                 for d in range(DIFF_TQ // DIFF_TK)]).astype(np.float32))


def _diff(qdt, kd1, kd2, vdt, scalars, slopes, mask, lam_p, subln_g, l):
    b, _, s = qdt.shape
    const = lambda arr: pl.BlockSpec(arr.shape, lambda bi, i: (0,) * arr.ndim, pipeline_mode=pl.Buffered(1))
    kspec = pl.BlockSpec((1, s, DIFF_HEADS * 2 * HEAD_DIM), lambda bi, i: (bi, 0, 0))
    return pl.pallas_call(
        functools.partial(_diff_kernel, layer=l),
        grid=(b, s // DIFF_TQ),
        in_specs=[
            pl.BlockSpec(memory_space=pltpu.SMEM),
            pl.BlockSpec((1, DIFF_HEADS * 2 * HEAD_DIM, DIFF_TQ), lambda bi, i: (bi, 0, i)),
            kspec,
            kspec,
            pl.BlockSpec((1,) + vdt.shape[1:], lambda bi, i: (bi, 0, 0, 0, 0)),
            const(slopes),
            const(mask),
            _layer_block(lam_p, l),
            _layer_block(subln_g, l),
        ],
        out_specs=pl.BlockSpec((1, DIFF_TQ, DIFF_HEADS * DIFF_VDIM), lambda bi, i: (bi, i, 0)),
        out_shape=jax.ShapeDtypeStruct((b, s, DIFF_HEADS * DIFF_VDIM), BF16),
        scratch_shapes=([pltpu.VMEM((DIFF_TK, DIFF_TQ), F32)] * 4
                        + [pltpu.VMEM((DIFF_VROWS, DIFF_TQ), F32)] * 2) * 2,
        compiler_params=_params(),
        name="diff_attn",
    )(scalars, qdt, kd1, kd2, vdt, slopes, mask, lam_p, subln_g)


def _mix_ffn_kernel(oa_ref, od_ref, sg_ref, x_ref, mod_ref, wbs_ref, wbd_ref, wo_ref, g_ref, wu_ref, cw_ref,
                    cb_ref, wd_ref, fg_ref, o_ref, act, carry, *, final):
    d = x_ref.shape[-1]
    d_ff = wd_ref.shape[0]
    tm = x_ref.shape[1]

    @pl.when(pl.program_id(1) == 0)
    def _():
        carry[...] = jnp.zeros_like(carry)

    pa = jnp.dot(oa_ref[0], wbs_ref[...], preferred_element_type=F32)
    pd = jnp.dot(od_ref[0], wbd_ref[...], preferred_element_type=F32)
    merged = sg_ref[0, :, :d].astype(F32) * pa + sg_ref[0, :, d:].astype(F32) * pd
    x = x_ref[0] + mod_ref[0, 2:3, :] * jnp.dot(merged.astype(BF16), wo_ref[...], preferred_element_type=F32)

    ms = jnp.mean(x * x, axis=-1, keepdims=True)
    y = x * lax.rsqrt(ms + RMS_EPS) * g_ref[...]
    h = y * (1.0 + mod_ref[0, 4:5, :]) + mod_ref[0, 3:4, :]
    hb = h.astype(BF16)

    c0 = 0
    for width in FF_CHUNKS:
        a = jnp.dot(hb, wu_ref[:, c0:c0 + width], preferred_element_type=F32)
        gate = jnp.dot(hb, wu_ref[:, d_ff + c0:d_ff + c0 + width], preferred_element_type=F32)
        full = jnp.concatenate([carry[:, c0:c0 + width], a], axis=0)
        carry[:, c0:c0 + width] = a[tm - 8:, :]
        a1 = pltpu.roll(full, 1, 0)[8:]
        a2 = pltpu.roll(full, 2, 0)[8:]
        z = (a2 * cw_ref[0:1, c0:c0 + width] + a1 * cw_ref[1:2, c0:c0 + width]
             + a * cw_ref[2:3, c0:c0 + width] + cb_ref[:, c0:c0 + width])
        gelu = 0.5 * z * (1.0 + lax.erf(z * (2.0 ** -0.5)))
        act[:, c0:c0 + width] = (gelu * gate).astype(BF16)
        c0 += width

    yd = jnp.dot(act[...], wd_ref[...], preferred_element_type=F32)
    out = x + mod_ref[0, 5:6, :] * yd
    if final:
        ms2 = jnp.mean(out * out, axis=-1, keepdims=True)
        out = out * lax.rsqrt(ms2 + RMS_EPS) * fg_ref[...]
    o_ref[0] = out


def _mix_ffn(out_a, out_d, sg, x, mod, w_bs, w_bd, w_o, norm_g, w_up, conv_w, conv_b, w_down, final_g, l, final):
    b, s, d = x.shape
    d_ff = w_down.shape[1]
    row = lambda bi, i: (bi, i, 0)
    return pl.pallas_call(
        functools.partial(_mix_ffn_kernel, final=final),
        grid=(b, s // TM),
        in_specs=[
            pl.BlockSpec((1, TM, out_a.shape[-1]), row),
            pl.BlockSpec((1, TM, out_d.shape[-1]), row),
            pl.BlockSpec((1, TM, 2 * d), row),
            pl.BlockSpec((1, TM, d), row),
            _mod_block(mod, l),
            _layer_block(w_bs, l),
            _layer_block(w_bd, l),
            _layer_block(w_o, l),
            _layer_block(norm_g, l),
            _layer_block(w_up, l),
            _layer_block(conv_w, l),
            _layer_block(conv_b, l),
            _layer_block(w_down, l),
            pl.BlockSpec((1, d), lambda bi, i: (0, 0)),
        ],
        out_specs=pl.BlockSpec((1, TM, d), row),
        out_shape=jax.ShapeDtypeStruct((b, s, d), F32),
        scratch_shapes=[pltpu.VMEM((TM, d_ff), BF16), pltpu.VMEM((8, d_ff), F32)],
        compiler_params=_params(),
        name="mix_ffn",
    )(out_a, out_d, sg, x, mod, w_bs, w_bd, w_o, norm_g, w_up, conv_w, conv_b, w_down, final_g.reshape(1, d))


def kernel(x, c, w_ada, b_ada, norm1_g, w_in, swa_sinks, diff_lambda, diff_subln_g, w_branch_swa,
           w_branch_diff, w_out, norm2_g, w_up, conv_w, conv_b, w_down, final_g):
    b, s, d = x.shape
    depth = w_ada.shape[0]

    c8 = jnp.zeros((8, d), F32).at[:b].set(c)
    mod = _ada(c8, w_ada, b_ada).reshape(depth, 8, 6, d)

    swa_bias = _swa_bias_table()
    diff_slopes = _diff_slope_table()
    diff_mask = _diff_mask_table()
    key_feat = _key_feature_table()
    lam_inits = [0.8 - 0.6 * math.exp(-0.3 * l) for l in range(depth)]
    diff_scalars = jnp.asarray([_alibi_slopes(DIFF_HEADS) + [li] for li in lam_inits], F32)

    w_in_b = w_in.astype(BF16)
    w_bs, w_bd, w_o = w_branch_swa.astype(BF16), w_branch_diff.astype(BF16), w_out.astype(BF16)
    w_up_b, w_down_b = w_up.astype(BF16), w_down.astype(BF16)
    n1g, n2g = norm1_g.reshape(depth, 1, d), norm2_g.reshape(depth, 1, d)
    subln_g = diff_subln_g.reshape(depth, 1, DIFF_VDIM)
    conv_b3 = conv_b.reshape(depth, 1, -1)

    for l in range(depth):
        qa, kva, qd, kd1, kd2, vd, sg = _in_proj(x, mod, n1g, w_in_b, key_feat, l)
        out_a = _swa(qa, kva, swa_sinks, swa_bias, l)
        out_d = _diff(qd, kd1, kd2, vd, diff_scalars, diff_slopes, diff_mask, diff_lambda, subln_g, l)
        x = _mix_ffn(out_a, out_d, sg, x, mod, w_bs, w_bd, w_o, n2g, w_up_b, conv_w, conv_b3, w_down_b, final_g,
                     l, final=(l == depth - 1))
    return x
```

```python
import functools
import math

import jax
import jax.numpy as jnp
import numpy as np
from jax import lax
from jax.experimental import pallas as pl
from jax.experimental.pallas import tpu as pltpu

F32 = jnp.float32
BF16 = jnp.bfloat16

HEAD_DIM = 64
SWA_HEADS = 8
SWA_KV_HEADS = 2
SWA_GROUP = SWA_HEADS // SWA_KV_HEADS
WINDOW = 128
BLOCK = 128
DIFF_HEADS = 4
DIFF_VDIM = 2 * HEAD_DIM
DIFF_VROWS = DIFF_VDIM + 16
CONV_WIDTH = 3
RMS_EPS = 1e-6
NEG_INF = -1e30
LOG2E = math.log2(math.e)

LANES = 128
VMEM_LIMIT_BYTES = 56 * 1024 * 1024

TM = 512
TM_IN = 1024
SWA_TQ = 512
DIFF_TQ = 512
DIFF_TK = 256
FF_CHUNKS = (768, 768, 768, 512)

NT_DIMS = (((1,), (1,)), ((), ()))


def _params(flags=None):
    return pltpu.CompilerParams(vmem_limit_bytes=VMEM_LIMIT_BYTES, flags=flags)


def _layer_block(arr, l):
    nd = arr.ndim - 1
    return pl.BlockSpec((None,) + arr.shape[1:], lambda *_: (l,) + (0,) * nd, pipeline_mode=pl.Buffered(1))


def _mod_block(mod, l):
    return pl.BlockSpec((None, 1) + mod.shape[2:], lambda bi, i: (l, bi, 0, 0))


def _alibi_slopes(n):
    return [2.0 ** (-8.0 * (i + 1) / n) for i in range(n)]


def _ada_kernel(c_ref, w_ref, b_ref, o_ref):
    c = c_ref[...]
    cs = c * jax.nn.sigmoid(c)
    o_ref[0] = jnp.dot(cs.astype(BF16), w_ref[0].astype(BF16), preferred_element_type=F32) + b_ref[0]


def _ada(c8, w_ada, b_ada):
    depth, d, six_d = w_ada.shape
    tn = 1024
    return pl.pallas_call(
        _ada_kernel,
        grid=(depth, six_d // tn),
        in_specs=[
            pl.BlockSpec((8, d), lambda l, j: (0, 0)),
            pl.BlockSpec((1, d, tn), lambda l, j: (l, 0, j)),
            pl.BlockSpec((1, 1, tn), lambda l, j: (l, 0, j)),
        ],
        out_specs=pl.BlockSpec((1, 8, tn), lambda l, j: (l, 0, j)),
        out_shape=jax.ShapeDtypeStruct((depth, 8, six_d), F32),
        compiler_params=_params(),
        name="ada_mod",
    )(c8, w_ada, b_ada.reshape(depth, 1, six_d))


IN_GROUPS = (("qa", 512), ("kva", 256), ("qd", 512), ("kd", 512), ("vd", 512), ("g", 2048))
IN_OFFSETS = {name: sum(w for _, w in IN_GROUPS[:j]) for j, (name, _) in enumerate(IN_GROUPS)}
IN_ORDER = ("g", "vd", "qd", "kd", "qa", "kva")
DIFF_FEATS = 3


def _in_proj_kernel(x_ref, mod_ref, g_ref, w_ref, feat_ref, qa_ref, kva_ref, qd_ref, kd1_ref, kd2_ref, vd_ref,
                    sg_ref):
    x = x_ref[0]
    ms = jnp.mean(x * x, axis=-1, keepdims=True)
    y = x * lax.rsqrt(ms + RMS_EPS) * g_ref[...]
    h = y * (1.0 + mod_ref[0, 1:2, :]) + mod_ref[0, 0:1, :]
    hb = h.astype(BF16)
    widths = dict(IN_GROUPS)
    for name in IN_ORDER:
        c0 = IN_OFFSETS[name]
        p = jnp.dot(hb, w_ref[:, c0:c0 + widths[name]], preferred_element_type=F32)
        if name == "qa":
            qa_ref[0] = (p * (HEAD_DIM ** -0.5 * LOG2E)).astype(BF16)
        elif name == "kva":
            kva_ref[0] = p.astype(BF16)
        elif name == "qd":
            qd_ref[0] = (p * (HEAD_DIM ** -0.5 * LOG2E)).T.astype(BF16)
        elif name == "kd":
            lane = lax.broadcasted_iota(jnp.int32, p.shape, 1)
            first_half = (lane & (2 * HEAD_DIM - 1)) < HEAD_DIM
            kd1_ref[0] = jnp.where(first_half, p, feat_ref[0]).astype(BF16)
            kd2_ref[0] = jnp.where(first_half, feat_ref[1], p).astype(BF16)
        elif name == "vd":
            pt = p.T
            ones = jnp.ones((DIFF_VROWS - DIFF_VDIM, DIFF_TK), BF16)
            for hd in range(DIFF_HEADS):
                for t in range(TM_IN // DIFF_TK):
                    vd_ref[0, hd, t, :DIFF_VDIM] = pt[hd * DIFF_VDIM:(hd + 1) * DIFF_VDIM,
                                                      t * DIFF_TK:(t + 1) * DIFF_TK].astype(BF16)
                    vd_ref[0, hd, t, DIFF_VDIM:] = ones
        else:
            sg_ref[0] = jax.nn.sigmoid(p).astype(BF16)


def _key_feature_table():
    r = (np.arange(TM_IN) % DIFF_TK).astype(np.float32)[:, None]
    lane = np.arange(DIFF_HEADS * 2 * HEAD_DIM)[None, :] % (2 * HEAD_DIM)
    f1 = np.where((lane >= HEAD_DIM) & (lane < HEAD_DIM + DIFF_FEATS), r, 0.0)
    f2 = np.where(lane < DIFF_FEATS, r, 0.0)
    return jnp.asarray(np.stack([f1, f2]).astype(np.float32))


def _in_proj(x, mod, norm_g, w_in, feat, l):
    b, s, d = x.shape
    row = lambda bi, i: (bi, i, 0)
    dq = DIFF_HEADS * 2 * HEAD_DIM
    plain = lambda w: (pl.BlockSpec((1, TM_IN, w), row), jax.ShapeDtypeStruct((b, s, w), BF16))
    outs = [
        plain(512),
        plain(256),
        (pl.BlockSpec((1, dq, TM_IN), lambda bi, i: (bi, 0, i)),
         jax.ShapeDtypeStruct((b, dq, s), BF16)),
        plain(dq),
        plain(dq),
        (pl.BlockSpec((1, DIFF_HEADS, TM_IN // DIFF_TK, DIFF_VROWS, DIFF_TK),
                      lambda bi, i: (bi, 0, i, 0, 0)),
         jax.ShapeDtypeStruct((b, DIFF_HEADS, s // DIFF_TK, DIFF_VROWS, DIFF_TK), BF16)),
        plain(2 * d),
    ]
    return pl.pallas_call(
        _in_proj_kernel,
        grid=(b, s // TM_IN),
        in_specs=[
            pl.BlockSpec((1, TM_IN, d), row),
            _mod_block(mod, l),
            _layer_block(norm_g, l),
            _layer_block(w_in, l),
            pl.BlockSpec(feat.shape, lambda bi, i: (0, 0, 0), pipeline_mode=pl.Buffered(1)),
        ],
        out_specs=[o[0] for o in outs],
        out_shape=[o[1] for o in outs],
        compiler_params=_params(),
        name="in_proj",
    )(x, mod, norm_g, w_in, feat)


def _swa_prepare(kc_ref, kp_ref, vc_ref, vp_ref, kcat, vcat):
    pr = lax.broadcasted_iota(jnp.int32, (LANES, LANES), 0)
    pc = lax.broadcasted_iota(jnp.int32, (LANES, LANES), 1)
    swap = jnp.where(pc == (pr + HEAD_DIM) % LANES, 1.0, 0.0).astype(BF16)
    low = lax.broadcasted_iota(jnp.int32, (SWA_TQ + BLOCK, LANES), 1) < HEAD_DIM
    k = jnp.concatenate([kp_ref[0], kc_ref[0]], axis=0)
    v = jnp.concatenate([vp_ref[0], vc_ref[0]], axis=0)
    ks = jnp.dot(k, swap, preferred_element_type=F32).astype(BF16)
    vs = jnp.dot(v, swap, preferred_element_type=F32).astype(BF16)
    zero = jnp.zeros_like(k)
    kcat[:, 0 * LANES:1 * LANES] = jnp.where(low, k, zero)
    kcat[:, 1 * LANES:2 * LANES] = jnp.where(low, zero, ks)
    kcat[:, 2 * LANES:3 * LANES] = jnp.where(low, ks, zero)
    kcat[:, 3 * LANES:4 * LANES] = jnp.where(low, zero, k)
    vcat[:, 0 * LANES:1 * LANES] = jnp.where(low, v, vs)
    vcat[:, 1 * LANES:2 * LANES] = jnp.where(low, vs, v)


def _swa_block(sb, layer, sink_ref, q_ref, bias_ref, o_ref, kcat, vcat):
    first = pl.program_id(1) == 0
    col = lax.broadcasted_iota(jnp.int32, (BLOCK, 2 * BLOCK), 1)
    lane = lax.broadcasted_iota(jnp.int32, (BLOCK, LANES), 1)
    r0 = sb * BLOCK
    for t in range(SWA_HEADS // 2):
        kh = t // (SWA_GROUP // 2)
        q_t = q_ref[0, r0:r0 + BLOCK, t * LANES:(t + 1) * LANES]
        v = vcat[r0:r0 + 2 * BLOCK, kh * LANES:(kh + 1) * LANES]
        halves = []
        for e in range(2):
            hq = 2 * t + e
            kx = kcat[r0:r0 + 2 * BLOCK, (2 * kh + e) * LANES:(2 * kh + e + 1) * LANES]
            s = lax.dot_general(q_t, kx, NT_DIMS, preferred_element_type=F32)
            s = s + bias_ref[hq]
            if sb == 0:
                s = jnp.where(first & (col < BLOCK), NEG_INF, s)
            sink = sink_ref[layer, hq] * LOG2E
            m = jnp.maximum(jnp.max(s, axis=-1, keepdims=True), sink)
            p = jnp.exp2(s - m)
            denom = jnp.sum(p, axis=-1, keepdims=True) + jnp.exp2(sink - m)
            o = jnp.dot(p.astype(BF16), v, preferred_element_type=F32)
            halves.append(o / denom)
        o_t = jnp.where(lane < HEAD_DIM, halves[0], halves[1])
        o_ref[0, r0:r0 + BLOCK, t * LANES:(t + 1) * LANES] = o_t.astype(BF16)


def _swa_bias_table():
    slopes = _alibi_slopes(SWA_HEADS)
    qi = np.arange(BLOCK)[:, None]
    kj = np.arange(2 * BLOCK)[None, :]
    delta = qi + BLOCK - kj
    valid = (delta >= 0) & (delta < WINDOW)
    tab = np.stack([np.where(valid, -sl * LOG2E * delta.astype(np.float64), NEG_INF) for sl in slopes])
    return jnp.asarray(tab.astype(np.float32))


def _attn_kernel(sc_ref, sink_ref, q_ref, k1_ref, k2_ref, v_ref, slope_ref, mask_ref, lam_ref, g_ref,
                 qa_ref, kc_ref, kp_ref, vc_ref, vp_ref, swa_bias_ref, o_ref, oa_ref, kcat, vcat, *scratch, layer):
    _swa_prepare(kc_ref, kp_ref, vc_ref, vp_ref, kcat, vcat)
    lam_init = sc_ref[layer, DIFF_HEADS]
    lp = lam_ref[...]
    lam = (jnp.exp(jnp.sum(lp[0:1] * lp[1:2], axis=-1, keepdims=True))
           - jnp.exp(jnp.sum(lp[2:3] * lp[3:4], axis=-1, keepdims=True)) + lam_init)
    per_set = len(scratch) // 2
    for hd in range(DIFF_HEADS):
        _diff_head(hd, lam, sc_ref[layer, hd] * LOG2E, lam_init, q_ref, k1_ref, k2_ref, v_ref, slope_ref, mask_ref,
                   g_ref, o_ref, *scratch[(hd % 2) * per_set:(hd % 2 + 1) * per_set])
        _swa_block(hd, layer, sink_ref, qa_ref, swa_bias_ref, oa_ref, kcat, vcat)


def _diff_head(hd, lam, slope2, lam_init, q_ref, k1_ref, k2_ref, v_ref, slope_ref, mask_ref, g_ref, o_ref,
               sa1, sa2, sb1, sb2, acc1, acc2):
    i = pl.program_id(1)
    hcols = slice(hd * 2 * HEAD_DIM, (hd + 1) * 2 * HEAD_DIM)
    qt = q_ref[0, hcols, :]
    qrow = lax.broadcasted_iota(jnp.int32, qt.shape, 0)
    q1 = jnp.where(qrow < HEAD_DIM, qt, slope_ref[hd, 0])
    q2 = jnp.where(qrow < HEAD_DIM, slope_ref[hd, 1], qt)
    tiles_per_q = DIFF_TQ // DIFF_TK
    first_diag = tiles_per_q * i
    n_tiles = first_diag + tiles_per_q

    acc1[...] = jnp.zeros_like(acc1)
    acc2[...] = jnp.zeros_like(acc2)

    def tile_at(n):
        return jnp.where(n < tiles_per_q, first_diag + n, n_tiles - 1 - n)

    def stage_a(t, s1, s2, diag=None):
        k0 = pl.multiple_of(t * DIFF_TK, DIFF_TK)
        x1 = jnp.dot(k1_ref[0, pl.ds(k0, DIFF_TK), hcols], q1, preferred_element_type=F32)
        x2 = jnp.dot(k2_ref[0, pl.ds(k0, DIFF_TK), hcols], q2, preferred_element_type=F32)
        if diag is not None:
            x1 = x1 + mask_ref[diag]
            x2 = x2 + mask_ref[diag]
        s1[...] = x1
        s2[...] = x2
        return jnp.max(x1, axis=0, keepdims=True), jnp.max(x2, axis=0, keepdims=True)

    def stage_b(t, mt, s, acc, m_old):
        cj = -slope2 * ((first_diag - t) * DIFF_TK).astype(F32)
        m_new = jnp.maximum(m_old, mt + cj)
        p = jnp.exp2(s[...] - (m_new - cj)).astype(BF16)
        vt = v_ref[0, hd, t]
        acc[...] = jnp.exp2(m_old - m_new) * acc[...] + jnp.dot(vt, p, preferred_element_type=F32)
        return m_new

    m0 = jnp.full((1, DIFF_TQ), NEG_INF, F32)
    mta1, mta2 = stage_a(first_diag, sa1, sa2, diag=0)
    mtb1, mtb2 = stage_a(first_diag + 1, sb1, sb2, diag=1)
    m1 = stage_b(first_diag, mta1, sa1, acc1, m0)
    m2 = stage_b(first_diag, mta2, sa2, acc2, m0)

    def body(g, carry):
        m1, m2, mtb1, mtb2 = carry
        n = 2 * g
        mta1, mta2 = stage_a(tile_at(n), sa1, sa2)
        m1 = stage_b(tile_at(n - 1), mtb1, sb1, acc1, m1)
        m2 = stage_b(tile_at(n - 1), mtb2, sb2, acc2, m2)
        mtb1, mtb2 = stage_a(tile_at(n + 1), sb1, sb2)
        m1 = stage_b(tile_at(n), mta1, sa1, acc1, m1)
        m2 = stage_b(tile_at(n), mta2, sa2, acc2, m2)
        return m1, m2, mtb1, mtb2

    n_pairs = n_tiles // 2 - 1
    odd = n_pairs % 2
    carry = lax.fori_loop(1, 1 + odd, body, (m1, m2, mtb1, mtb2))
    m1, m2, mtb1, mtb2 = lax.fori_loop(
        0, n_pairs // 2, lambda j, c: body(2 + odd + 2 * j, body(1 + odd + 2 * j, c)), carry)
    stage_b(tile_at(n_tiles - 1), mtb1, sb1, acc1, m1)
    stage_b(tile_at(n_tiles - 1), mtb2, sb2, acc2, m2)

    l1 = acc1[DIFF_VDIM:DIFF_VDIM + 1, :]
    l2 = acc2[DIFF_VDIM:DIFF_VDIM + 1, :]
    ot = acc1[:DIFF_VDIM, :] / l1 - lam * (acc2[:DIFF_VDIM, :] / l2)
    ms = jnp.mean(ot * ot, axis=0, keepdims=True)
    yt = ot * lax.rsqrt(ms + RMS_EPS)
    o_ref[0, :, hcols] = (yt.T * g_ref[...] * (1.0 - lam_init)).astype(BF16)


def _diff_slope_table():
    tab = np.zeros((DIFF_HEADS, 2, 2 * HEAD_DIM, DIFF_TQ), np.float32)
    for hd, sl in enumerate(_alibi_slopes(DIFF_HEADS)):
        rest = np.float32(sl * LOG2E)
        for j in range(DIFF_FEATS):
            piece = np.float32(rest.astype(jnp.bfloat16))
            tab[hd, 0, HEAD_DIM + j, :] = piece
            tab[hd, 1, j, :] = piece
            rest = np.float32(rest - piece)
    return jnp.asarray(tab.astype(jnp.bfloat16))


def _diff_mask_table():
    kr = np.arange(DIFF_TK)[:, None]
    qc = np.arange(DIFF_TQ)[None, :]
    return jnp.asarray(np.stack([np.where(kr + d * DIFF_TK <= qc, 0.0, NEG_INF)
                                 for d in range(DIFF_TQ // DIFF_TK)]).astype(np.float32))


def _attention(qdt, kd1, kd2, vdt, scalars, slopes, mask, lam_p, subln_g, qa, kva, sinks, swa_bias, l):
    b, _, s = qdt.shape
    assert SWA_TQ == DIFF_TQ and SWA_TQ // BLOCK == DIFF_HEADS
    const = lambda arr: pl.BlockSpec(arr.shape, lambda bi, i: (0,) * arr.ndim, pipeline_mode=pl.Buffered(1))
    kspec = pl.BlockSpec((1, s, DIFF_HEADS * 2 * HEAD_DIM), lambda bi, i: (bi, 0, 0))
    row = lambda bi, i: (bi, i, 0)
    prev = lambda i: jnp.maximum(i * (SWA_TQ // BLOCK) - 1, 0)
    out_d, out_a = pl.pallas_call(
        functools.partial(_attn_kernel, layer=l),
        grid=(b, s // DIFF_TQ),
        in_specs=[
            pl.BlockSpec(memory_space=pltpu.SMEM),
            pl.BlockSpec(memory_space=pltpu.SMEM),
            pl.BlockSpec((1, DIFF_HEADS * 2 * HEAD_DIM, DIFF_TQ), lambda bi, i: (bi, 0, i)),
            kspec,
            kspec,
            pl.BlockSpec((1,) + vdt.shape[1:], lambda bi, i: (bi, 0, 0, 0, 0)),
            const(slopes),
            const(mask),
            _layer_block(lam_p, l),
            _layer_block(subln_g, l),
            pl.BlockSpec((1, SWA_TQ, SWA_HEADS * HEAD_DIM), row),
            pl.BlockSpec((1, SWA_TQ, LANES), lambda bi, i: (bi, i, 0)),
            pl.BlockSpec((1, BLOCK, LANES), lambda bi, i: (bi, prev(i), 0)),
            pl.BlockSpec((1, SWA_TQ, LANES), lambda bi, i: (bi, i, 1)),
            pl.BlockSpec((1, BLOCK, LANES), lambda bi, i: (bi, prev(i), 1)),
            const(swa_bias),
        ],
        out_specs=[pl.BlockSpec((1, DIFF_TQ, DIFF_HEADS * DIFF_VDIM), row),
                   pl.BlockSpec((1, SWA_TQ, SWA_HEADS * HEAD_DIM), row)],
        out_shape=[jax.ShapeDtypeStruct((b, s, DIFF_HEADS * DIFF_VDIM), BF16),
                   jax.ShapeDtypeStruct((b, s, SWA_HEADS * HEAD_DIM), BF16)],
        scratch_shapes=([pltpu.VMEM((SWA_TQ + BLOCK, 4 * LANES), BF16), pltpu.VMEM((SWA_TQ + BLOCK, 2 * LANES), BF16)]
                        + ([pltpu.VMEM((DIFF_TK, DIFF_TQ), F32)] * 4
                           + [pltpu.VMEM((DIFF_VROWS, DIFF_TQ), F32)] * 2) * 2),
        compiler_params=_params(),
        name="attention",
    )(scalars, sinks, qdt, kd1, kd2, vdt, slopes, mask, lam_p, subln_g, qa, kva, kva, kva, kva, swa_bias)
    return out_a, out_d


def _mix_ffn_kernel(oa_ref, od_ref, sg_ref, x_ref, mod_ref, wbs_ref, wbd_ref, wo_ref, g_ref, wu_ref, cw_ref,
                    cb_ref, wd_ref, fg_ref, o_ref, act, carry, *, final):
    d = x_ref.shape[-1]
    d_ff = wd_ref.shape[0]
    tm = x_ref.shape[1]

    @pl.when(pl.program_id(1) == 0)
    def _():
        carry[...] = jnp.zeros_like(carry)

    pa = jnp.dot(oa_ref[0], wbs_ref[...], preferred_element_type=F32)
    pd = jnp.dot(od_ref[0], wbd_ref[...], preferred_element_type=F32)
    merged = sg_ref[0, :, :d].astype(F32) * pa + sg_ref[0, :, d:].astype(F32) * pd
    x = x_ref[0] + mod_ref[0, 2:3, :] * jnp.dot(merged.astype(BF16), wo_ref[...], preferred_element_type=F32)

    ms = jnp.mean(x * x, axis=-1, keepdims=True)
    y = x * lax.rsqrt(ms + RMS_EPS) * g_ref[...]
    h = y * (1.0 + mod_ref[0, 4:5, :]) + mod_ref[0, 3:4, :]
    hb = h.astype(BF16)

    c0 = 0
    for width in FF_CHUNKS:
        a = jnp.dot(hb, wu_ref[:, c0:c0 + width], preferred_element_type=F32)
        gate = jnp.dot(hb, wu_ref[:, d_ff + c0:d_ff + c0 + width], preferred_element_type=F32)
        full = jnp.concatenate([carry[:, c0:c0 + width], a], axis=0)
        carry[:, c0:c0 + width] = a[tm - 8:, :]
        a1 = pltpu.roll(full, 1, 0)[8:]
        a2 = pltpu.roll(full, 2, 0)[8:]
        z = (a2 * cw_ref[0:1, c0:c0 + width] + a1 * cw_ref[1:2, c0:c0 + width]
             + a * cw_ref[2:3, c0:c0 + width] + cb_ref[:, c0:c0 + width])
        gelu = 0.5 * z * (1.0 + lax.erf(z * (2.0 ** -0.5)))
        act[:, c0:c0 + width] = (gelu * gate).astype(BF16)
        c0 += width

    yd = jnp.dot(act[...], wd_ref[...], preferred_element_type=F32)
    out = x + mod_ref[0, 5:6, :] * yd
    if final:
        ms2 = jnp.mean(out * out, axis=-1, keepdims=True)
        out = out * lax.rsqrt(ms2 + RMS_EPS) * fg_ref[...]
    o_ref[0] = out


def _mix_ffn(out_a, out_d, sg, x, mod, w_bs, w_bd, w_o, norm_g, w_up, conv_w, conv_b, w_down, final_g, l, final):
    b, s, d = x.shape
    d_ff = w_down.shape[1]
    row = lambda bi, i: (bi, i, 0)
    return pl.pallas_call(
        functools.partial(_mix_ffn_kernel, final=final),
        grid=(b, s // TM),
        in_specs=[
            pl.BlockSpec((1, TM, out_a.shape[-1]), row),
            pl.BlockSpec((1, TM, out_d.shape[-1]), row),
            pl.BlockSpec((1, TM, 2 * d), row),
            pl.BlockSpec((1, TM, d), row),
            _mod_block(mod, l),
            _layer_block(w_bs, l),
            _layer_block(w_bd, l),
            _layer_block(w_o, l),
            _layer_block(norm_g, l),
            _layer_block(w_up, l),
            _layer_block(conv_w, l),
            _layer_block(conv_b, l),
            _layer_block(w_down, l),
            pl.BlockSpec((1, d), lambda bi, i: (0, 0)),
        ],
        out_specs=pl.BlockSpec((1, TM, d), row),
        out_shape=jax.ShapeDtypeStruct((b, s, d), F32),
        scratch_shapes=[pltpu.VMEM((TM, d_ff), BF16), pltpu.VMEM((8, d_ff), F32)],
        compiler_params=_params(),
        name="mix_ffn",
    )(out_a, out_d, sg, x, mod, w_bs, w_bd, w_o, norm_g, w_up, conv_w, conv_b, w_down, final_g.reshape(1, d))


def kernel(x, c, w_ada, b_ada, norm1_g, w_in, swa_sinks, diff_lambda, diff_subln_g, w_branch_swa,
           w_branch_diff, w_out, norm2_g, w_up, conv_w, conv_b, w_down, final_g):
    b, s, d = x.shape
    depth = w_ada.shape[0]

    c8 = jnp.zeros((8, d), F32).at[:b].set(c)
    mod = _ada(c8, w_ada, b_ada).reshape(depth, 8, 6, d)

    swa_bias = _swa_bias_table()
    diff_slopes = _diff_slope_table()
    diff_mask = _diff_mask_table()
    key_feat = _key_feature_table()
    lam_inits = [0.8 - 0.6 * math.exp(-0.3 * l) for l in range(depth)]
    diff_scalars = jnp.asarray([_alibi_slopes(DIFF_HEADS) + [li] for li in lam_inits], F32)

    w_in_b = w_in.astype(BF16)
    w_bs, w_bd, w_o = w_branch_swa.astype(BF16), w_branch_diff.astype(BF16), w_out.astype(BF16)
    w_up_b, w_down_b = w_up.astype(BF16), w_down.astype(BF16)
    n1g, n2g = norm1_g.reshape(depth, 1, d), norm2_g.reshape(depth, 1, d)
    subln_g = diff_subln_g.reshape(depth, 1, DIFF_VDIM)
    conv_b3 = conv_b.reshape(depth, 1, -1)

    for l in range(depth):
        qa, kva, qd, kd1, kd2, vd, sg = _in_proj(x, mod, n1g, w_in_b, key_feat, l)
        out_a, out_d = _attention(qd, kd1, kd2, vd, diff_scalars, diff_slopes, diff_mask, diff_lambda, subln_g,
                                  qa, kva, swa_sinks, swa_bias, l)
        x = _mix_ffn(out_a, out_d, sg, x, mod, w_bs, w_bd, w_o, n2g, w_up_b, conv_w, conv_b3, w_down_b, final_g,
                     l, final=(l == depth - 1))
    return x
```

```python
import functools
import math

import jax
import jax.numpy as jnp
import numpy as np
from jax import lax
from jax.experimental import pallas as pl
from jax.experimental.pallas import tpu as pltpu

F32 = jnp.float32
BF16 = jnp.bfloat16

HEAD_DIM = 64
SWA_HEADS = 8
SWA_KV_HEADS = 2
SWA_GROUP = SWA_HEADS // SWA_KV_HEADS
WINDOW = 128
BLOCK = 128
DIFF_HEADS = 4
DIFF_VDIM = 2 * HEAD_DIM
DIFF_VROWS = DIFF_VDIM + 16
CONV_WIDTH = 3
RMS_EPS = 1e-6
NEG_INF = -1e30
LOG2E = math.log2(math.e)

LANES = 128
VMEM_LIMIT_BYTES = 56 * 1024 * 1024

TM = 512
TM_IN = 1024
SWA_TQ = 512
DIFF_TQ = 512
DIFF_TK = 256
FF_CHUNKS = (768, 768, 768, 512)

NT_DIMS = (((1,), (1,)), ((), ()))


def _params(flags=None):
    return pltpu.CompilerParams(vmem_limit_bytes=VMEM_LIMIT_BYTES, flags=flags)


def _layer_block(arr, l):
    nd = arr.ndim - 1
    return pl.BlockSpec((None,) + arr.shape[1:], lambda *_: (l,) + (0,) * nd, pipeline_mode=pl.Buffered(1))


def _mod_block(mod, l):
    return pl.BlockSpec((None, 1) + mod.shape[2:], lambda bi, i: (l, bi, 0, 0))


def _alibi_slopes(n):
    return [2.0 ** (-8.0 * (i + 1) / n) for i in range(n)]


def _ada_kernel(c_ref, w_ref, b_ref, o_ref):
    c = c_ref[...]
    cs = c * jax.nn.sigmoid(c)
    o_ref[0] = jnp.dot(cs.astype(BF16), w_ref[0].astype(BF16), preferred_element_type=F32) + b_ref[0]


def _ada(c8, w_ada, b_ada):
    depth, d, six_d = w_ada.shape
    tn = 1024
    return pl.pallas_call(
        _ada_kernel,
        grid=(depth, six_d // tn),
        in_specs=[
            pl.BlockSpec((8, d), lambda l, j: (0, 0)),
            pl.BlockSpec((1, d, tn), lambda l, j: (l, 0, j)),
            pl.BlockSpec((1, 1, tn), lambda l, j: (l, 0, j)),
        ],
        out_specs=pl.BlockSpec((1, 8, tn), lambda l, j: (l, 0, j)),
        out_shape=jax.ShapeDtypeStruct((depth, 8, six_d), F32),
        compiler_params=_params(),
        name="ada_mod",
    )(c8, w_ada, b_ada.reshape(depth, 1, six_d))


IN_GROUPS = (("qa", 512), ("kva", 256), ("qd", 512), ("kd", 512), ("vd", 512), ("g", 2048))
IN_OFFSETS = {name: sum(w for _, w in IN_GROUPS[:j]) for j, (name, _) in enumerate(IN_GROUPS)}
IN_ORDER = ("g", "vd", "qd", "kd", "qa", "kva")
DIFF_FEATS = 3


N_CAST = 5


def _in_proj_kernel(x_ref, mod_ref, g_ref, w_ref, feat_ref, *refs):
    cast_in, refs = refs[:N_CAST], refs[N_CAST:]
    (qa_ref, kva_ref, qd_ref, kd1_ref, kd2_ref, vd_ref, sg_ref), cast_out = refs[:-N_CAST], refs[-N_CAST:]
    for src, dst in zip(cast_in, cast_out):
        dst[...] = src[...].astype(BF16)
    x = x_ref[0]
    ms = jnp.mean(x * x, axis=-1, keepdims=True)
    y = x * lax.rsqrt(ms + RMS_EPS) * g_ref[...]
    h = y * (1.0 + mod_ref[0, 1:2, :]) + mod_ref[0, 0:1, :]
    hb = h.astype(BF16)
    widths = dict(IN_GROUPS)
    for name in IN_ORDER:
        c0 = IN_OFFSETS[name]
        p = jnp.dot(hb, w_ref[:, c0:c0 + widths[name]], preferred_element_type=F32)
        if name == "qa":
            qa_ref[0] = (p * (HEAD_DIM ** -0.5 * LOG2E)).astype(BF16)
        elif name == "kva":
            kva_ref[0] = p.astype(BF16)
        elif name == "qd":
            qd_ref[0] = (p * (HEAD_DIM ** -0.5 * LOG2E)).T.astype(BF16)
        elif name == "kd":
            lane = lax.broadcasted_iota(jnp.int32, p.shape, 1)
            first_half = (lane & (2 * HEAD_DIM - 1)) < HEAD_DIM
            kd1_ref[0] = jnp.where(first_half, p, feat_ref[0]).astype(BF16)
            kd2_ref[0] = jnp.where(first_half, feat_ref[1], p).astype(BF16)
        elif name == "vd":
            pt = p.T
            ones = jnp.ones((DIFF_VROWS - DIFF_VDIM, DIFF_TK), BF16)
            for hd in range(DIFF_HEADS):
                for t in range(TM_IN // DIFF_TK):
                    vd_ref[0, hd, t, :DIFF_VDIM] = pt[hd * DIFF_VDIM:(hd + 1) * DIFF_VDIM,
                                                      t * DIFF_TK:(t + 1) * DIFF_TK].astype(BF16)
                    vd_ref[0, hd, t, DIFF_VDIM:] = ones
        else:
            sg_ref[0] = jax.nn.sigmoid(p).astype(BF16)


def _key_feature_table():
    r = (np.arange(TM_IN) % DIFF_TK).astype(np.float32)[:, None]
    lane = np.arange(DIFF_HEADS * 2 * HEAD_DIM)[None, :] % (2 * HEAD_DIM)
    f1 = np.where((lane >= HEAD_DIM) & (lane < HEAD_DIM + DIFF_FEATS), r, 0.0)
    f2 = np.where(lane < DIFF_FEATS, r, 0.0)
    return jnp.asarray(np.stack([f1, f2]).astype(np.float32))


def _in_proj(x, mod, norm_g, w_in, feat, cast_ws, l):
    b, s, d = x.shape
    assert len(cast_ws) == N_CAST
    steps_per_b = s // TM_IN
    n_steps = b * steps_per_b
    cast_in_specs = [pl.BlockSpec((None, w.shape[1] // n_steps, w.shape[2]),
                                  lambda bi, i: (l, bi * steps_per_b + i, 0)) for w in cast_ws]
    cast_out = [(pl.BlockSpec((w.shape[1] // n_steps, w.shape[2]), lambda bi, i: (bi * steps_per_b + i, 0)),
                 jax.ShapeDtypeStruct(w.shape[1:], BF16)) for w in cast_ws]
    row = lambda bi, i: (bi, i, 0)
    dq = DIFF_HEADS * 2 * HEAD_DIM
    plain = lambda w: (pl.BlockSpec((1, TM_IN, w), row), jax.ShapeDtypeStruct((b, s, w), BF16))
    outs = [
        plain(512),
        plain(256),
        (pl.BlockSpec((1, dq, TM_IN), lambda bi, i: (bi, 0, i)),
         jax.ShapeDtypeStruct((b, dq, s), BF16)),
        plain(dq),
        plain(dq),
        (pl.BlockSpec((1, DIFF_HEADS, TM_IN // DIFF_TK, DIFF_VROWS, DIFF_TK),
                      lambda bi, i: (bi, 0, i, 0, 0)),
         jax.ShapeDtypeStruct((b, DIFF_HEADS, s // DIFF_TK, DIFF_VROWS, DIFF_TK), BF16)),
        plain(2 * d),
    ]
    res = pl.pallas_call(
        _in_proj_kernel,
        grid=(b, s // TM_IN),
        in_specs=[
            pl.BlockSpec((1, TM_IN, d), row),
            _mod_block(mod, l),
            _layer_block(norm_g, l),
            pl.BlockSpec(w_in.shape, lambda bi, i: (0, 0), pipeline_mode=pl.Buffered(1)),
            pl.BlockSpec(feat.shape, lambda bi, i: (0, 0, 0), pipeline_mode=pl.Buffered(1)),
        ] + cast_in_specs,
        out_specs=[o[0] for o in outs + cast_out],
        out_shape=[o[1] for o in outs + cast_out],
        compiler_params=_params(),
        name="in_proj",
    )(x, mod, norm_g, w_in, feat, *cast_ws)
    return res[:len(outs)], res[len(outs):]


def _swa_prepare(kc_ref, kp_ref, vc_ref, vp_ref, kcat, vcat):
    pr = lax.broadcasted_iota(jnp.int32, (LANES, LANES), 0)
    pc = lax.broadcasted_iota(jnp.int32, (LANES, LANES), 1)
    swap = jnp.where(pc == (pr + HEAD_DIM) % LANES, 1.0, 0.0).astype(BF16)
    low = lax.broadcasted_iota(jnp.int32, (SWA_TQ + BLOCK, LANES), 1) < HEAD_DIM
    k = jnp.concatenate([kp_ref[0], kc_ref[0]], axis=0)
    v = jnp.concatenate([vp_ref[0], vc_ref[0]], axis=0)
    ks = jnp.dot(k, swap, preferred_element_type=F32).astype(BF16)
    vs = jnp.dot(v, swap, preferred_element_type=F32).astype(BF16)
    zero = jnp.zeros_like(k)
    kcat[:, 0 * LANES:1 * LANES] = jnp.where(low, k, zero)
    kcat[:, 1 * LANES:2 * LANES] = jnp.where(low, zero, ks)
    kcat[:, 2 * LANES:3 * LANES] = jnp.where(low, ks, zero)
    kcat[:, 3 * LANES:4 * LANES] = jnp.where(low, zero, k)
    vcat[:, 0 * LANES:1 * LANES] = jnp.where(low, v, vs)
    vcat[:, 1 * LANES:2 * LANES] = jnp.where(low, vs, v)


def _swa_block(sb, layer, sink_ref, q_ref, bias_ref, o_ref, kcat, vcat):
    first = pl.program_id(1) == 0
    col = lax.broadcasted_iota(jnp.int32, (BLOCK, 2 * BLOCK), 1)
    lane = lax.broadcasted_iota(jnp.int32, (BLOCK, LANES), 1)
    r0 = sb * BLOCK
    for t in range(SWA_HEADS // 2):
        kh = t // (SWA_GROUP // 2)
        q_t = q_ref[0, r0:r0 + BLOCK, t * LANES:(t + 1) * LANES]
        v = vcat[r0:r0 + 2 * BLOCK, kh * LANES:(kh + 1) * LANES]
        halves = []
        for e in range(2):
            hq = 2 * t + e
            kx = kcat[r0:r0 + 2 * BLOCK, (2 * kh + e) * LANES:(2 * kh + e + 1) * LANES]
            s = lax.dot_general(q_t, kx, NT_DIMS, preferred_element_type=F32)
            s = s + bias_ref[hq]
            if sb == 0:
                s = jnp.where(first & (col < BLOCK), NEG_INF, s)
            sink = sink_ref[layer, hq] * LOG2E
            m = jnp.maximum(jnp.max(s, axis=-1, keepdims=True), sink)
            p = jnp.exp2(s - m)
            denom = jnp.sum(p, axis=-1, keepdims=True) + jnp.exp2(sink - m)
            o = jnp.dot(p.astype(BF16), v, preferred_element_type=F32)
            halves.append(o / denom)
        o_t = jnp.where(lane < HEAD_DIM, halves[0], halves[1])
        o_ref[0, r0:r0 + BLOCK, t * LANES:(t + 1) * LANES] = o_t.astype(BF16)


def _swa_bias_table():
    slopes = _alibi_slopes(SWA_HEADS)
    qi = np.arange(BLOCK)[:, None]
    kj = np.arange(2 * BLOCK)[None, :]
    delta = qi + BLOCK - kj
    valid = (delta >= 0) & (delta < WINDOW)
    tab = np.stack([np.where(valid, -sl * LOG2E * delta.astype(np.float64), NEG_INF) for sl in slopes])
    return jnp.asarray(tab.astype(np.float32))


def _attn_kernel(sc_ref, sink_ref, q_ref, k1_ref, k2_ref, v_ref, slope_ref, mask_ref, lam_ref, g_ref,
                 qa_ref, kc_ref, kp_ref, vc_ref, vp_ref, swa_bias_ref, o_ref, oa_ref, kcat, vcat, *scratch, layer):
    _swa_prepare(kc_ref, kp_ref, vc_ref, vp_ref, kcat, vcat)
    lam_init = sc_ref[layer, DIFF_HEADS]
    lp = lam_ref[...]
    lam = (jnp.exp(jnp.sum(lp[0:1] * lp[1:2], axis=-1, keepdims=True))
           - jnp.exp(jnp.sum(lp[2:3] * lp[3:4], axis=-1, keepdims=True)) + lam_init)
    per_set = len(scratch) // 2
    for hd in range(DIFF_HEADS):
        _diff_head(hd, lam, sc_ref[layer, hd] * LOG2E, lam_init, q_ref, k1_ref, k2_ref, v_ref, slope_ref, mask_ref,
                   g_ref, o_ref, *scratch[(hd % 2) * per_set:(hd % 2 + 1) * per_set])
        _swa_block(hd, layer, sink_ref, qa_ref, swa_bias_ref, oa_ref, kcat, vcat)


def _diff_head(hd, lam, slope2, lam_init, q_ref, k1_ref, k2_ref, v_ref, slope_ref, mask_ref, g_ref, o_ref,
               sa1, sa2, sb1, sb2, acc1, acc2):
    i = pl.program_id(1)
    hcols = slice(hd * 2 * HEAD_DIM, (hd + 1) * 2 * HEAD_DIM)
    qt = q_ref[0, hcols, :]
    qrow = lax.broadcasted_iota(jnp.int32, qt.shape, 0)
    q1 = jnp.where(qrow < HEAD_DIM, qt, slope_ref[hd, 0])
    q2 = jnp.where(qrow < HEAD_DIM, slope_ref[hd, 1], qt)
    tiles_per_q = DIFF_TQ // DIFF_TK
    first_diag = tiles_per_q * i
    n_tiles = first_diag + tiles_per_q

    acc1[...] = jnp.zeros_like(acc1)
    acc2[...] = jnp.zeros_like(acc2)

    def tile_at(n):
        return jnp.where(n < tiles_per_q, first_diag + n, n_tiles - 1 - n)

    def stage_a(t, s1, s2, diag=None):
        k0 = pl.multiple_of(t * DIFF_TK, DIFF_TK)
        x1 = jnp.dot(k1_ref[0, pl.ds(k0, DIFF_TK), hcols], q1, preferred_element_type=F32)
        x2 = jnp.dot(k2_ref[0, pl.ds(k0, DIFF_TK), hcols], q2, preferred_element_type=F32)
        if diag is not None:
            x1 = x1 + mask_ref[diag]
            x2 = x2 + mask_ref[diag]
        s1[...] = x1
        s2[...] = x2
        return jnp.max(x1, axis=0, keepdims=True), jnp.max(x2, axis=0, keepdims=True)

    def stage_b(t, mt, s, acc, m_old):
        cj = -slope2 * ((first_diag - t) * DIFF_TK).astype(F32)
        m_new = jnp.maximum(m_old, mt + cj)
        p = jnp.exp2(s[...] - (m_new - cj)).astype(BF16)
        vt = v_ref[0, hd, t]
        acc[...] = jnp.exp2(m_old - m_new) * acc[...] + jnp.dot(vt, p, preferred_element_type=F32)
        return m_new

    m0 = jnp.full((1, DIFF_TQ), NEG_INF, F32)
    mta1, mta2 = stage_a(first_diag, sa1, sa2, diag=0)
    mtb1, mtb2 = stage_a(first_diag + 1, sb1, sb2, diag=1)
    m1 = stage_b(first_diag, mta1, sa1, acc1, m0)
    m2 = stage_b(first_diag, mta2, sa2, acc2, m0)

    def body(g, carry):
        m1, m2, mtb1, mtb2 = carry
        n = 2 * g
        mta1, mta2 = stage_a(tile_at(n), sa1, sa2)
        m1 = stage_b(tile_at(n - 1), mtb1, sb1, acc1, m1)
        m2 = stage_b(tile_at(n - 1), mtb2, sb2, acc2, m2)
        mtb1, mtb2 = stage_a(tile_at(n + 1), sb1, sb2)
        m1 = stage_b(tile_at(n), mta1, sa1, acc1, m1)
        m2 = stage_b(tile_at(n), mta2, sa2, acc2, m2)
        return m1, m2, mtb1, mtb2

    n_pairs = n_tiles // 2 - 1
    odd = n_pairs % 2
    carry = lax.fori_loop(1, 1 + odd, body, (m1, m2, mtb1, mtb2))
    m1, m2, mtb1, mtb2 = lax.fori_loop(
        0, n_pairs // 2, lambda j, c: body(2 + odd + 2 * j, body(1 + odd + 2 * j, c)), carry)
    stage_b(tile_at(n_tiles - 1), mtb1, sb1, acc1, m1)
    stage_b(tile_at(n_tiles - 1), mtb2, sb2, acc2, m2)

    l1 = acc1[DIFF_VDIM:DIFF_VDIM + 1, :]
    l2 = acc2[DIFF_VDIM:DIFF_VDIM + 1, :]
    ot = acc1[:DIFF_VDIM, :] / l1 - lam * (acc2[:DIFF_VDIM, :] / l2)
    ms = jnp.mean(ot * ot, axis=0, keepdims=True)
    yt = ot * lax.rsqrt(ms + RMS_EPS)
    o_ref[0, :, hcols] = (yt.T * g_ref[...] * (1.0 - lam_init)).astype(BF16)


def _diff_slope_table():
    tab = np.zeros((DIFF_HEADS, 2, 2 * HEAD_DIM, DIFF_TQ), np.float32)
    for hd, sl in enumerate(_alibi_slopes(DIFF_HEADS)):
        rest = np.float32(sl * LOG2E)
        for j in range(DIFF_FEATS):
            piece = np.float32(rest.astype(jnp.bfloat16))
            tab[hd, 0, HEAD_DIM + j, :] = piece
            tab[hd, 1, j, :] = piece
            rest = np.float32(rest - piece)
    return jnp.asarray(tab.astype(jnp.bfloat16))


def _diff_mask_table():
    kr = np.arange(DIFF_TK)[:, None]
    qc = np.arange(DIFF_TQ)[None, :]
    return jnp.asarray(np.stack([np.where(kr + d * DIFF_TK <= qc, 0.0, NEG_INF)
                                 for d in range(DIFF_TQ // DIFF_TK)]).astype(np.float32))


def _attention(qdt, kd1, kd2, vdt, scalars, slopes, mask, lam_p, subln_g, qa, kva, sinks, swa_bias, l):
    b, _, s = qdt.shape
    assert SWA_TQ == DIFF_TQ and SWA_TQ // BLOCK == DIFF_HEADS
    const = lambda arr: pl.BlockSpec(arr.shape, lambda bi, i: (0,) * arr.ndim, pipeline_mode=pl.Buffered(1))
    kspec = pl.BlockSpec((1, s, DIFF_HEADS * 2 * HEAD_DIM), lambda bi, i: (bi, 0, 0))
    row = lambda bi, i: (bi, i, 0)
    prev = lambda i: jnp.maximum(i * (SWA_TQ // BLOCK) - 1, 0)
    out_d, out_a = pl.pallas_call(
        functools.partial(_attn_kernel, layer=l),
        grid=(b, s // DIFF_TQ),
        in_specs=[
            pl.BlockSpec(memory_space=pltpu.SMEM),
            pl.BlockSpec(memory_space=pltpu.SMEM),
            pl.BlockSpec((1, DIFF_HEADS * 2 * HEAD_DIM, DIFF_TQ), lambda bi, i: (bi, 0, i)),
            kspec,
            kspec,
            pl.BlockSpec((1,) + vdt.shape[1:], lambda bi, i: (bi, 0, 0, 0, 0)),
            const(slopes),
            const(mask),
            _layer_block(lam_p, l),
            _layer_block(subln_g, l),
            pl.BlockSpec((1, SWA_TQ, SWA_HEADS * HEAD_DIM), row),
            pl.BlockSpec((1, SWA_TQ, LANES), lambda bi, i: (bi, i, 0)),
            pl.BlockSpec((1, BLOCK, LANES), lambda bi, i: (bi, prev(i), 0)),
            pl.BlockSpec((1, SWA_TQ, LANES), lambda bi, i: (bi, i, 1)),
            pl.BlockSpec((1, BLOCK, LANES), lambda bi, i: (bi, prev(i), 1)),
            const(swa_bias),
        ],
        out_specs=[pl.BlockSpec((1, DIFF_TQ, DIFF_HEADS * DIFF_VDIM), row),
                   pl.BlockSpec((1, SWA_TQ, SWA_HEADS * HEAD_DIM), row)],
        out_shape=[jax.ShapeDtypeStruct((b, s, DIFF_HEADS * DIFF_VDIM), BF16),
                   jax.ShapeDtypeStruct((b, s, SWA_HEADS * HEAD_DIM), BF16)],
        scratch_shapes=([pltpu.VMEM((SWA_TQ + BLOCK, 4 * LANES), BF16), pltpu.VMEM((SWA_TQ + BLOCK, 2 * LANES), BF16)]
                        + ([pltpu.VMEM((DIFF_TK, DIFF_TQ), F32)] * 4
                           + [pltpu.VMEM((DIFF_VROWS, DIFF_TQ), F32)] * 2) * 2),
        compiler_params=_params(),
        name="attention",
    )(scalars, sinks, qdt, kd1, kd2, vdt, slopes, mask, lam_p, subln_g, qa, kva, kva, kva, kva, swa_bias)
    return out_a, out_d


def _mix_ffn_kernel(oa_ref, od_ref, sg_ref, x_ref, mod_ref, wbs_ref, wbd_ref, wo_ref, g_ref, wu_ref, cw_ref,
                    cb_ref, wd_ref, fg_ref, win_ref, o_ref, win_out_ref, act, carry, *, final):
    win_out_ref[...] = win_ref[...].astype(BF16)
    d = x_ref.shape[-1]
    d_ff = wd_ref.shape[0]
    tm = x_ref.shape[1]

    @pl.when(pl.program_id(1) == 0)
    def _():
        carry[...] = jnp.zeros_like(carry)

    pa = jnp.dot(oa_ref[0], wbs_ref[...], preferred_element_type=F32)
    pd = jnp.dot(od_ref[0], wbd_ref[...], preferred_element_type=F32)
    merged = sg_ref[0, :, :d].astype(F32) * pa + sg_ref[0, :, d:].astype(F32) * pd
    x = x_ref[0] + mod_ref[0, 2:3, :] * jnp.dot(merged.astype(BF16), wo_ref[...], preferred_element_type=F32)

    ms = jnp.mean(x * x, axis=-1, keepdims=True)
    y = x * lax.rsqrt(ms + RMS_EPS) * g_ref[...]
    h = y * (1.0 + mod_ref[0, 4:5, :]) + mod_ref[0, 3:4, :]
    hb = h.astype(BF16)

    c0 = 0
    for width in FF_CHUNKS:
        a = jnp.dot(hb, wu_ref[:, c0:c0 + width], preferred_element_type=F32)
        gate = jnp.dot(hb, wu_ref[:, d_ff + c0:d_ff + c0 + width], preferred_element_type=F32)
        full = jnp.concatenate([carry[:, c0:c0 + width], a], axis=0)
        carry[:, c0:c0 + width] = a[tm - 8:, :]
        a1 = pltpu.roll(full, 1, 0)[8:]
        a2 = pltpu.roll(full, 2, 0)[8:]
        z = (a2 * cw_ref[0:1, c0:c0 + width] + a1 * cw_ref[1:2, c0:c0 + width]
             + a * cw_ref[2:3, c0:c0 + width] + cb_ref[:, c0:c0 + width])
        gelu = 0.5 * z * (1.0 + lax.erf(z * (2.0 ** -0.5)))
        act[:, c0:c0 + width] = (gelu * gate).astype(BF16)
        c0 += width

    yd = jnp.dot(act[...], wd_ref[...], preferred_element_type=F32)
    out = x + mod_ref[0, 5:6, :] * yd
    if final:
        ms2 = jnp.mean(out * out, axis=-1, keepdims=True)
        out = out * lax.rsqrt(ms2 + RMS_EPS) * fg_ref[...]
    o_ref[0] = out


def _mix_ffn(out_a, out_d, sg, x, mod, w_bs, w_bd, w_o, norm_g, w_up, conv_w, conv_b, w_down, final_g, w_in,
             l, final):
    b, s, d = x.shape
    d_ff = w_down.shape[0]
    depth, win_rows, win_cols = w_in.shape
    steps_per_b = s // TM
    chunk = win_rows // (b * steps_per_b)
    row = lambda bi, i: (bi, i, 0)
    whole = lambda w: pl.BlockSpec(w.shape, lambda bi, i: (0, 0), pipeline_mode=pl.Buffered(1))
    return pl.pallas_call(
        functools.partial(_mix_ffn_kernel, final=final),
        grid=(b, s // TM),
        in_specs=[
            pl.BlockSpec((1, TM, out_a.shape[-1]), row),
            pl.BlockSpec((1, TM, out_d.shape[-1]), row),
            pl.BlockSpec((1, TM, 2 * d), row),
            pl.BlockSpec((1, TM, d), row),
            _mod_block(mod, l),
            whole(w_bs),
            whole(w_bd),
            whole(w_o),
            _layer_block(norm_g, l),
            whole(w_up),
            _layer_block(conv_w, l),
            _layer_block(conv_b, l),
            whole(w_down),
            pl.BlockSpec((1, d), lambda bi, i: (0, 0)),
            pl.BlockSpec((None, chunk, win_cols), lambda bi, i: ((l + 1) % depth, bi * steps_per_b + i, 0)),
        ],
        out_specs=[pl.BlockSpec((1, TM, d), row),
                   pl.BlockSpec((chunk, win_cols), lambda bi, i: (bi * steps_per_b + i, 0))],
        out_shape=[jax.ShapeDtypeStruct((b, s, d), F32), jax.ShapeDtypeStruct((win_rows, win_cols), BF16)],
        scratch_shapes=[pltpu.VMEM((TM, d_ff), BF16), pltpu.VMEM((8, d_ff), F32)],
        compiler_params=_params(),
        name="mix_ffn",
    )(out_a, out_d, sg, x, mod, w_bs, w_bd, w_o, norm_g, w_up, conv_w, conv_b, w_down, final_g.reshape(1, d), w_in)


def kernel(x, c, w_ada, b_ada, norm1_g, w_in, swa_sinks, diff_lambda, diff_subln_g, w_branch_swa,
           w_branch_diff, w_out, norm2_g, w_up, conv_w, conv_b, w_down, final_g):
    b, s, d = x.shape
    depth = w_ada.shape[0]

    c8 = jnp.zeros((8, d), F32).at[:b].set(c)
    mod = _ada(c8, w_ada, b_ada).reshape(depth, 8, 6, d)

    swa_bias = _swa_bias_table()
    diff_slopes = _diff_slope_table()
    diff_mask = _diff_mask_table()
    key_feat = _key_feature_table()
    lam_inits = [0.8 - 0.6 * math.exp(-0.3 * l) for l in range(depth)]
    diff_scalars = jnp.asarray([_alibi_slopes(DIFF_HEADS) + [li] for li in lam_inits], F32)

    w_in_b = w_in[0].astype(BF16)
    later_ws = (w_branch_swa, w_branch_diff, w_out, w_up, w_down)
    n1g, n2g = norm1_g.reshape(depth, 1, d), norm2_g.reshape(depth, 1, d)
    subln_g = diff_subln_g.reshape(depth, 1, DIFF_VDIM)
    conv_b3 = conv_b.reshape(depth, 1, -1)

    for l in range(depth):
        (qa, kva, qd, kd1, kd2, vd, sg), (w_bs, w_bd, w_o, w_up_b, w_down_b) = _in_proj(
            x, mod, n1g, w_in_b, key_feat, later_ws, l)
        out_a, out_d = _attention(qd, kd1, kd2, vd, diff_scalars, diff_slopes, diff_mask, diff_lambda, subln_g,
                                  qa, kva, swa_sinks, swa_bias, l)
        x, w_in_b = _mix_ffn(out_a, out_d, sg, x, mod, w_bs, w_bd, w_o, n2g, w_up_b, conv_w, conv_b3, w_down_b,
                             final_g, w_in, l, final=(l == depth - 1))
    return x
```

```python
import functools
import math

import jax
import jax.numpy as jnp
import numpy as np
from jax import lax
from jax.experimental import pallas as pl
from jax.experimental.pallas import tpu as pltpu

F32 = jnp.float32
BF16 = jnp.bfloat16

HEAD_DIM = 64
SWA_HEADS = 8
SWA_KV_HEADS = 2
SWA_GROUP = SWA_HEADS // SWA_KV_HEADS
WINDOW = 128
BLOCK = 128
DIFF_HEADS = 4
DIFF_VDIM = 2 * HEAD_DIM
DIFF_VROWS = DIFF_VDIM + 16
CONV_WIDTH = 3
RMS_EPS = 1e-6
NEG_INF = -1e30
LOG2E = math.log2(math.e)

LANES = 128
VMEM_LIMIT_BYTES = 56 * 1024 * 1024

TM = 512
TM_IN = 1024
SWA_TQ = 512
DIFF_TQ = 512
DIFF_TK = 256
FF_CHUNKS = (768, 768, 768, 512)

NT_DIMS = (((1,), (1,)), ((), ()))


def _params(flags=None):
    return pltpu.CompilerParams(vmem_limit_bytes=VMEM_LIMIT_BYTES, flags=flags)


def _layer_block(arr, l):
    nd = arr.ndim - 1
    return pl.BlockSpec((None,) + arr.shape[1:], lambda *_: (l,) + (0,) * nd, pipeline_mode=pl.Buffered(1))


def _mod_block(mod, l):
    return pl.BlockSpec((None, 1) + mod.shape[2:], lambda bi, i: (l, bi, 0, 0))


def _alibi_slopes(n):
    return [2.0 ** (-8.0 * (i + 1) / n) for i in range(n)]


def _ada_kernel(c_ref, w_ref, b_ref, o_ref):
    c = c_ref[...]
    cs = c * jax.nn.sigmoid(c)
    o_ref[0] = jnp.dot(cs.astype(BF16), w_ref[0].astype(BF16), preferred_element_type=F32) + b_ref[0]


def _ada(c8, w_ada, b_ada):
    depth, d, six_d = w_ada.shape
    tn = 1024
    return pl.pallas_call(
        _ada_kernel,
        grid=(depth, six_d // tn),
        in_specs=[
            pl.BlockSpec((8, d), lambda l, j: (0, 0)),
            pl.BlockSpec((1, d, tn), lambda l, j: (l, 0, j)),
            pl.BlockSpec((1, 1, tn), lambda l, j: (l, 0, j)),
        ],
        out_specs=pl.BlockSpec((1, 8, tn), lambda l, j: (l, 0, j)),
        out_shape=jax.ShapeDtypeStruct((depth, 8, six_d), F32),
        compiler_params=_params(),
        name="ada_mod",
    )(c8, w_ada, b_ada.reshape(depth, 1, six_d))


IN_GROUPS = (("qa", 512), ("kva", 256), ("qd", 512), ("kd", 512), ("vd", 512), ("g", 2048))
IN_OFFSETS = {name: sum(w for _, w in IN_GROUPS[:j]) for j, (name, _) in enumerate(IN_GROUPS)}
IN_ORDER = ("g", "vd", "qd", "kd", "qa", "kva")
DIFF_FEATS = 3


N_CAST = 5


def _in_proj_kernel(x_ref, mod_ref, g_ref, w_ref, feat_ref, *refs):
    cast_in, refs = refs[:N_CAST], refs[N_CAST:]
    (qa_ref, kva_ref, qd_ref, kd1_ref, kd2_ref, vd_ref, sg_ref), cast_out = refs[:-N_CAST], refs[-N_CAST:]
    for src, dst in zip(cast_in, cast_out):
        dst[...] = src[...].astype(BF16)
    x = x_ref[0]
    ms = jnp.mean(x * x, axis=-1, keepdims=True)
    y = x * lax.rsqrt(ms + RMS_EPS) * g_ref[...]
    h = y * (1.0 + mod_ref[0, 1:2, :]) + mod_ref[0, 0:1, :]
    hb = h.astype(BF16)
    widths = dict(IN_GROUPS)
    for name in IN_ORDER:
        c0 = IN_OFFSETS[name]
        p = jnp.dot(hb, w_ref[:, c0:c0 + widths[name]], preferred_element_type=F32)
        if name == "qa":
            qa_ref[0] = (p * (HEAD_DIM ** -0.5 * LOG2E)).astype(BF16)
        elif name == "kva":
            kva_ref[0] = p.astype(BF16)
        elif name == "qd":
            qd_ref[0] = (p * (HEAD_DIM ** -0.5 * LOG2E)).T.astype(BF16)
        elif name == "kd":
            lane = lax.broadcasted_iota(jnp.int32, p.shape, 1)
            first_half = (lane & (2 * HEAD_DIM - 1)) < HEAD_DIM
            kd1_ref[0] = jnp.where(first_half, p, feat_ref[0]).astype(BF16)
            kd2_ref[0] = jnp.where(first_half, feat_ref[1], p).astype(BF16)
        elif name == "vd":
            pt = p.T
            ones = jnp.ones((DIFF_VROWS - DIFF_VDIM, DIFF_TK), BF16)
            for hd in range(DIFF_HEADS):
                for t in range(TM_IN // DIFF_TK):
                    vd_ref[0, hd, t, :DIFF_VDIM] = pt[hd * DIFF_VDIM:(hd + 1) * DIFF_VDIM,
                                                      t * DIFF_TK:(t + 1) * DIFF_TK].astype(BF16)
                    vd_ref[0, hd, t, DIFF_VDIM:] = ones
        else:
            sg_ref[0] = jax.nn.sigmoid(p).astype(BF16)


def _key_feature_table():
    r = (np.arange(TM_IN) % DIFF_TK).astype(np.float32)[:, None]
    lane = np.arange(DIFF_HEADS * 2 * HEAD_DIM)[None, :] % (2 * HEAD_DIM)
    f1 = np.where((lane >= HEAD_DIM) & (lane < HEAD_DIM + DIFF_FEATS), r, 0.0)
    f2 = np.where(lane < DIFF_FEATS, r, 0.0)
    return jnp.asarray(np.stack([f1, f2]).astype(np.float32))


def _in_proj(x, mod, norm_g, w_in, feat, cast_ws, l):
    b, s, d = x.shape
    assert len(cast_ws) == N_CAST
    steps_per_b = s // TM_IN
    n_steps = b * steps_per_b
    cast_in_specs = [pl.BlockSpec((None, w.shape[1] // n_steps, w.shape[2]),
                                  lambda bi, i: (l, bi * steps_per_b + i, 0)) for w in cast_ws]
    cast_out = [(pl.BlockSpec((w.shape[1] // n_steps, w.shape[2]), lambda bi, i: (bi * steps_per_b + i, 0)),
                 jax.ShapeDtypeStruct(w.shape[1:], BF16)) for w in cast_ws]
    row = lambda bi, i: (bi, i, 0)
    dq = DIFF_HEADS * 2 * HEAD_DIM
    plain = lambda w: (pl.BlockSpec((1, TM_IN, w), row), jax.ShapeDtypeStruct((b, s, w), BF16))
    outs = [
        plain(512),
        plain(256),
        (pl.BlockSpec((1, dq, TM_IN), lambda bi, i: (bi, 0, i)),
         jax.ShapeDtypeStruct((b, dq, s), BF16)),
        plain(dq),
        plain(dq),
        (pl.BlockSpec((1, DIFF_HEADS, TM_IN // DIFF_TK, DIFF_VROWS, DIFF_TK),
                      lambda bi, i: (bi, 0, i, 0, 0)),
         jax.ShapeDtypeStruct((b, DIFF_HEADS, s // DIFF_TK, DIFF_VROWS, DIFF_TK), BF16)),
        plain(2 * d),
    ]
    res = pl.pallas_call(
        _in_proj_kernel,
        grid=(b, s // TM_IN),
        in_specs=[
            pl.BlockSpec((1, TM_IN, d), row),
            _mod_block(mod, l),
            _layer_block(norm_g, l),
            pl.BlockSpec(w_in.shape, lambda bi, i: (0, 0), pipeline_mode=pl.Buffered(1)),
            pl.BlockSpec(feat.shape, lambda bi, i: (0, 0, 0), pipeline_mode=pl.Buffered(1)),
        ] + cast_in_specs,
        out_specs=[o[0] for o in outs + cast_out],
        out_shape=[o[1] for o in outs + cast_out],
        compiler_params=_params(),
        name="in_proj",
    )(x, mod, norm_g, w_in, feat, *cast_ws)
    return res[:len(outs)], res[len(outs):]


def _swa_prepare(kc_ref, kp_ref, vc_ref, vp_ref, kcat, vcat):
    pr = lax.broadcasted_iota(jnp.int32, (LANES, LANES), 0)
    pc = lax.broadcasted_iota(jnp.int32, (LANES, LANES), 1)
    swap = jnp.where(pc == (pr + HEAD_DIM) % LANES, 1.0, 0.0).astype(BF16)
    low = lax.broadcasted_iota(jnp.int32, (SWA_TQ + BLOCK, LANES), 1) < HEAD_DIM
    k = jnp.concatenate([kp_ref[0], kc_ref[0]], axis=0)
    v = jnp.concatenate([vp_ref[0], vc_ref[0]], axis=0)
    ks = jnp.dot(k, swap, preferred_element_type=F32).astype(BF16)
    vs = jnp.dot(v, swap, preferred_element_type=F32).astype(BF16)
    zero = jnp.zeros_like(k)
    kcat[:, 0 * LANES:1 * LANES] = jnp.where(low, k, zero)
    kcat[:, 1 * LANES:2 * LANES] = jnp.where(low, zero, ks)
    kcat[:, 2 * LANES:3 * LANES] = jnp.where(low, ks, zero)
    kcat[:, 3 * LANES:4 * LANES] = jnp.where(low, zero, k)
    vcat[:, 0 * LANES:1 * LANES] = jnp.where(low, v, vs)
    vcat[:, 1 * LANES:2 * LANES] = jnp.where(low, vs, v)


def _swa_block(sb, layer, sink_ref, q_ref, bias_ref, o_ref, kcat, vcat):
    first = pl.program_id(1) == 0
    col = lax.broadcasted_iota(jnp.int32, (BLOCK, 2 * BLOCK), 1)
    lane = lax.broadcasted_iota(jnp.int32, (BLOCK, LANES), 1)
    r0 = sb * BLOCK
    for t in range(SWA_HEADS // 2):
        kh = t // (SWA_GROUP // 2)
        q_t = q_ref[0, r0:r0 + BLOCK, t * LANES:(t + 1) * LANES]
        v = vcat[r0:r0 + 2 * BLOCK, kh * LANES:(kh + 1) * LANES]
        halves = []
        for e in range(2):
            hq = 2 * t + e
            kx = kcat[r0:r0 + 2 * BLOCK, (2 * kh + e) * LANES:(2 * kh + e + 1) * LANES]
            s = lax.dot_general(q_t, kx, NT_DIMS, preferred_element_type=F32)
            s = s + bias_ref[hq]
            if sb == 0:
                s = jnp.where(first & (col < BLOCK), NEG_INF, s)
            sink = sink_ref[layer, hq] * LOG2E
            m = jnp.maximum(jnp.max(s, axis=-1, keepdims=True), sink)
            p = jnp.exp2(s - m)
            denom = jnp.sum(p, axis=-1, keepdims=True) + jnp.exp2(sink - m)
            o = jnp.dot(p.astype(BF16), v, preferred_element_type=F32)
            halves.append(o / denom)
        o_t = jnp.where(lane < HEAD_DIM, halves[0], halves[1])
        o_ref[0, r0:r0 + BLOCK, t * LANES:(t + 1) * LANES] = o_t.astype(BF16)


def _swa_bias_table():
    slopes = _alibi_slopes(SWA_HEADS)
    qi = np.arange(BLOCK)[:, None]
    kj = np.arange(2 * BLOCK)[None, :]
    delta = qi + BLOCK - kj
    valid = (delta >= 0) & (delta < WINDOW)
    tab = np.stack([np.where(valid, -sl * LOG2E * delta.astype(np.float64), NEG_INF) for sl in slopes])
    return jnp.asarray(tab.astype(np.float32))


def _attn_kernel(sc_ref, sink_ref, q_ref, k1_ref, k2_ref, v_ref, slope_ref, mask_ref, lam_ref, g_ref,
                 qa_ref, kc_ref, kp_ref, vc_ref, vp_ref, swa_bias_ref, o_ref, oa_ref, kcat, vcat, *scratch, layer):
    _swa_prepare(kc_ref, kp_ref, vc_ref, vp_ref, kcat, vcat)
    lam_init = sc_ref[layer, DIFF_HEADS]
    lp = lam_ref[...]
    lam = (jnp.exp(jnp.sum(lp[0:1] * lp[1:2], axis=-1, keepdims=True))
           - jnp.exp(jnp.sum(lp[2:3] * lp[3:4], axis=-1, keepdims=True)) + lam_init)
    per_set = len(scratch) // 2
    for h0 in range(0, DIFF_HEADS, 2):
        _diff_heads([(hd, sc_ref[layer, hd] * LOG2E, scratch[(hd % 2) * per_set:(hd % 2 + 1) * per_set])
                     for hd in (h0, h0 + 1)],
                    lam, lam_init, q_ref, k1_ref, k2_ref, v_ref, slope_ref, mask_ref, g_ref, o_ref)
        for hd in (h0, h0 + 1):
            _swa_block(hd, layer, sink_ref, qa_ref, swa_bias_ref, oa_ref, kcat, vcat)


def _diff_heads(heads, lam, lam_init, q_ref, k1_ref, k2_ref, v_ref, slope_ref, mask_ref, g_ref, o_ref):
    i = pl.program_id(1)
    tiles_per_q = DIFF_TQ // DIFF_TK
    first_diag = tiles_per_q * i
    n_tiles = first_diag + tiles_per_q

    def tile_at(n):
        return jnp.where(n < tiles_per_q, first_diag + n, n_tiles - 1 - n)

    def make(hd, slope2, sa1, sa2, sb1, sb2, acc1, acc2):
        hcols = slice(hd * 2 * HEAD_DIM, (hd + 1) * 2 * HEAD_DIM)
        qt = q_ref[0, hcols, :]
        qrow = lax.broadcasted_iota(jnp.int32, qt.shape, 0)
        q1 = jnp.where(qrow < HEAD_DIM, qt, slope_ref[hd, 0])
        q2 = jnp.where(qrow < HEAD_DIM, slope_ref[hd, 1], qt)

        def stage_a(t, s1, s2, diag=None):
            k0 = pl.multiple_of(t * DIFF_TK, DIFF_TK)
            x1 = jnp.dot(k1_ref[0, pl.ds(k0, DIFF_TK), hcols], q1, preferred_element_type=F32)
            x2 = jnp.dot(k2_ref[0, pl.ds(k0, DIFF_TK), hcols], q2, preferred_element_type=F32)
            if diag is not None:
                x1 = x1 + mask_ref[diag]
                x2 = x2 + mask_ref[diag]
            s1[...] = x1
            s2[...] = x2
            return jnp.max(x1, axis=0, keepdims=True), jnp.max(x2, axis=0, keepdims=True)

        def stage_b(t, mt, s, acc, m_old):
            cj = -slope2 * ((first_diag - t) * DIFF_TK).astype(F32)
            m_new = jnp.maximum(m_old, mt + cj)
            p = jnp.exp2(s[...] - (m_new - cj)).astype(BF16)
            vt = v_ref[0, hd, t]
            acc[...] = jnp.exp2(m_old - m_new) * acc[...] + jnp.dot(vt, p, preferred_element_type=F32)
            return m_new

        def prologue():
            acc1[...] = jnp.zeros_like(acc1)
            acc2[...] = jnp.zeros_like(acc2)
            m0 = jnp.full((1, DIFF_TQ), NEG_INF, F32)
            mta1, mta2 = stage_a(first_diag, sa1, sa2, diag=0)
            mtb1, mtb2 = stage_a(first_diag + 1, sb1, sb2, diag=1)
            return (stage_b(first_diag, mta1, sa1, acc1, m0), stage_b(first_diag, mta2, sa2, acc2, m0),
                    mtb1, mtb2)

        def body(g, carry):
            m1, m2, mtb1, mtb2 = carry
            n = 2 * g
            mta1, mta2 = stage_a(tile_at(n), sa1, sa2)
            m1 = stage_b(tile_at(n - 1), mtb1, sb1, acc1, m1)
            m2 = stage_b(tile_at(n - 1), mtb2, sb2, acc2, m2)
            mtb1, mtb2 = stage_a(tile_at(n + 1), sb1, sb2)
            m1 = stage_b(tile_at(n), mta1, sa1, acc1, m1)
            m2 = stage_b(tile_at(n), mta2, sa2, acc2, m2)
            return m1, m2, mtb1, mtb2

        def epilogue(carry):
            m1, m2, mtb1, mtb2 = carry
            stage_b(tile_at(n_tiles - 1), mtb1, sb1, acc1, m1)
            stage_b(tile_at(n_tiles - 1), mtb2, sb2, acc2, m2)
            l1 = acc1[DIFF_VDIM:DIFF_VDIM + 1, :]
            l2 = acc2[DIFF_VDIM:DIFF_VDIM + 1, :]
            ot = acc1[:DIFF_VDIM, :] / l1 - lam * (acc2[:DIFF_VDIM, :] / l2)
            ms = jnp.mean(ot * ot, axis=0, keepdims=True)
            yt = ot * lax.rsqrt(ms + RMS_EPS)
            o_ref[0, :, hcols] = (yt.T * g_ref[...] * (1.0 - lam_init)).astype(BF16)

        return prologue, body, epilogue

    stages = [make(hd, slope2, *scr) for hd, slope2, scr in heads]
    carry = tuple(prologue() for prologue, _, _ in stages)

    def body_all(g, carry):
        return tuple(body(g, c) for (_, body, _), c in zip(stages, carry))

    n_pairs = n_tiles // 2 - 1
    odd = n_pairs % 2
    carry = lax.fori_loop(1, 1 + odd, body_all, carry)
    carry = lax.fori_loop(
        0, n_pairs // 2, lambda j, c: body_all(2 + odd + 2 * j, body_all(1 + odd + 2 * j, c)), carry)
    for (_, _, epilogue), c in zip(stages, carry):
        epilogue(c)


def _diff_slope_table():
    tab = np.zeros((DIFF_HEADS, 2, 2 * HEAD_DIM, DIFF_TQ), np.float32)
    for hd, sl in enumerate(_alibi_slopes(DIFF_HEADS)):
        rest = np.float32(sl * LOG2E)
        for j in range(DIFF_FEATS):
            piece = np.float32(rest.astype(jnp.bfloat16))
            tab[hd, 0, HEAD_DIM + j, :] = piece
            tab[hd, 1, j, :] = piece
            rest = np.float32(rest - piece)
    return jnp.asarray(tab.astype(jnp.bfloat16))


def _diff_mask_table():
    kr = np.arange(DIFF_TK)[:, None]
    qc = np.arange(DIFF_TQ)[None, :]
    return jnp.asarray(np.stack([np.where(kr + d * DIFF_TK <= qc, 0.0, NEG_INF)
                                 for d in range(DIFF_TQ // DIFF_TK)]).astype(np.float32))


def _attention(qdt, kd1, kd2, vdt, scalars, slopes, mask, lam_p, subln_g, qa, kva, sinks, swa_bias, l):
    b, _, s = qdt.shape
    assert SWA_TQ == DIFF_TQ and SWA_TQ // BLOCK == DIFF_HEADS
    const = lambda arr: pl.BlockSpec(arr.shape, lambda bi, i: (0,) * arr.ndim, pipeline_mode=pl.Buffered(1))
    kspec = pl.BlockSpec((1, s, DIFF_HEADS * 2 * HEAD_DIM), lambda bi, i: (bi, 0, 0))
    row = lambda bi, i: (bi, i, 0)
    prev = lambda i: jnp.maximum(i * (SWA_TQ // BLOCK) - 1, 0)
    out_d, out_a = pl.pallas_call(
        functools.partial(_attn_kernel, layer=l),
        grid=(b, s // DIFF_TQ),
        in_specs=[
            pl.BlockSpec(memory_space=pltpu.SMEM),
            pl.BlockSpec(memory_space=pltpu.SMEM),
            pl.BlockSpec((1, DIFF_HEADS * 2 * HEAD_DIM, DIFF_TQ), lambda bi, i: (bi, 0, i)),
            kspec,
            kspec,
            pl.BlockSpec((1,) + vdt.shape[1:], lambda bi, i: (bi, 0, 0, 0, 0)),
            const(slopes),
            const(mask),
            _layer_block(lam_p, l),
            _layer_block(subln_g, l),
            pl.BlockSpec((1, SWA_TQ, SWA_HEADS * HEAD_DIM), row),
            pl.BlockSpec((1, SWA_TQ, LANES), lambda bi, i: (bi, i, 0)),
            pl.BlockSpec((1, BLOCK, LANES), lambda bi, i: (bi, prev(i), 0)),
            pl.BlockSpec((1, SWA_TQ, LANES), lambda bi, i: (bi, i, 1)),
            pl.BlockSpec((1, BLOCK, LANES), lambda bi, i: (bi, prev(i), 1)),
            const(swa_bias),
        ],
        out_specs=[pl.BlockSpec((1, DIFF_TQ, DIFF_HEADS * DIFF_VDIM), row),
                   pl.BlockSpec((1, SWA_TQ, SWA_HEADS * HEAD_DIM), row)],
        out_shape=[jax.ShapeDtypeStruct((b, s, DIFF_HEADS * DIFF_VDIM), BF16),
                   jax.ShapeDtypeStruct((b, s, SWA_HEADS * HEAD_DIM), BF16)],
        scratch_shapes=([pltpu.VMEM((SWA_TQ + BLOCK, 4 * LANES), BF16), pltpu.VMEM((SWA_TQ + BLOCK, 2 * LANES), BF16)]
                        + ([pltpu.VMEM((DIFF_TK, DIFF_TQ), F32)] * 4
                           + [pltpu.VMEM((DIFF_VROWS, DIFF_TQ), F32)] * 2) * 2),
        compiler_params=_params(),
        name="attention",
    )(scalars, sinks, qdt, kd1, kd2, vdt, slopes, mask, lam_p, subln_g, qa, kva, kva, kva, kva, swa_bias)
    return out_a, out_d


def _mix_ffn_kernel(oa_ref, od_ref, sg_ref, x_ref, mod_ref, wbs_ref, wbd_ref, wo_ref, g_ref, wu_ref, cw_ref,
                    cb_ref, wd_ref, fg_ref, win_ref, o_ref, win_out_ref, act, carry, *, final):
    win_out_ref[...] = win_ref[...].astype(BF16)
    d = x_ref.shape[-1]
    d_ff = wd_ref.shape[0]
    tm = x_ref.shape[1]

    @pl.when(pl.program_id(1) == 0)
    def _():
        carry[...] = jnp.zeros_like(carry)

    pa = jnp.dot(oa_ref[0], wbs_ref[...], preferred_element_type=F32)
    pd = jnp.dot(od_ref[0], wbd_ref[...], preferred_element_type=F32)
    merged = sg_ref[0, :, :d].astype(F32) * pa + sg_ref[0, :, d:].astype(F32) * pd
    x = x_ref[0] + mod_ref[0, 2:3, :] * jnp.dot(merged.astype(BF16), wo_ref[...], preferred_element_type=F32)

    ms = jnp.mean(x * x, axis=-1, keepdims=True)
    y = x * lax.rsqrt(ms + RMS_EPS) * g_ref[...]
    h = y * (1.0 + mod_ref[0, 4:5, :]) + mod_ref[0, 3:4, :]
    hb = h.astype(BF16)

    c0 = 0
    for width in FF_CHUNKS:
        a = jnp.dot(hb, wu_ref[:, c0:c0 + width], preferred_element_type=F32)
        gate = jnp.dot(hb, wu_ref[:, d_ff + c0:d_ff + c0 + width], preferred_element_type=F32)
        full = jnp.concatenate([carry[:, c0:c0 + width], a], axis=0)
        carry[:, c0:c0 + width] = a[tm - 8:, :]
        a1 = pltpu.roll(full, 1, 0)[8:]
        a2 = pltpu.roll(full, 2, 0)[8:]
        z = (a2 * cw_ref[0:1, c0:c0 + width] + a1 * cw_ref[1:2, c0:c0 + width]
             + a * cw_ref[2:3, c0:c0 + width] + cb_ref[:, c0:c0 + width])
        gelu = 0.5 * z * (1.0 + lax.erf(z * (2.0 ** -0.5)))
        act[:, c0:c0 + width] = (gelu * gate).astype(BF16)
        c0 += width

    yd = jnp.dot(act[...], wd_ref[...], preferred_element_type=F32)
    out = x + mod_ref[0, 5:6, :] * yd
    if final:
        ms2 = jnp.mean(out * out, axis=-1, keepdims=True)
        out = out * lax.rsqrt(ms2 + RMS_EPS) * fg_ref[...]
    o_ref[0] = out


def _mix_ffn(out_a, out_d, sg, x, mod, w_bs, w_bd, w_o, norm_g, w_up, conv_w, conv_b, w_down, final_g, w_in,
             l, final):
    b, s, d = x.shape
    d_ff = w_down.shape[0]
    depth, win_rows, win_cols = w_in.shape
    steps_per_b = s // TM
    chunk = win_rows // (b * steps_per_b)
    row = lambda bi, i: (bi, i, 0)
    whole = lambda w: pl.BlockSpec(w.shape, lambda bi, i: (0, 0), pipeline_mode=pl.Buffered(1))
    return pl.pallas_call(
        functools.partial(_mix_ffn_kernel, final=final),
        grid=(b, s // TM),
        in_specs=[
            pl.BlockSpec((1, TM, out_a.shape[-1]), row),
            pl.BlockSpec((1, TM, out_d.shape[-1]), row),
            pl.BlockSpec((1, TM, 2 * d), row),
            pl.BlockSpec((1, TM, d), row),
            _mod_block(mod, l),
            whole(w_bs),
            whole(w_bd),
            whole(w_o),
            _layer_block(norm_g, l),
            whole(w_up),
            _layer_block(conv_w, l),
            _layer_block(conv_b, l),
            whole(w_down),
            pl.BlockSpec((1, d), lambda bi, i: (0, 0)),
            pl.BlockSpec((None, chunk, win_cols), lambda bi, i: ((l + 1) % depth, bi * steps_per_b + i, 0)),
        ],
        out_specs=[pl.BlockSpec((1, TM, d), row),
                   pl.BlockSpec((chunk, win_cols), lambda bi, i: (bi * steps_per_b + i, 0))],
        out_shape=[jax.ShapeDtypeStruct((b, s, d), F32), jax.ShapeDtypeStruct((win_rows, win_cols), BF16)],
        scratch_shapes=[pltpu.VMEM((TM, d_ff), BF16), pltpu.VMEM((8, d_ff), F32)],
        compiler_params=_params(),
        name="mix_ffn",
    )(out_a, out_d, sg, x, mod, w_bs, w_bd, w_o, norm_g, w_up, conv_w, conv_b, w_down, final_g.reshape(1, d), w_in)


def kernel(x, c, w_ada, b_ada, norm1_g, w_in, swa_sinks, diff_lambda, diff_subln_g, w_branch_swa,
           w_branch_diff, w_out, norm2_g, w_up, conv_w, conv_b, w_down, final_g):
    b, s, d = x.shape
    depth = w_ada.shape[0]

    c8 = jnp.zeros((8, d), F32).at[:b].set(c)
    mod = _ada(c8, w_ada, b_ada).reshape(depth, 8, 6, d)

    swa_bias = _swa_bias_table()
    diff_slopes = _diff_slope_table()
    diff_mask = _diff_mask_table()
    key_feat = _key_feature_table()
    lam_inits = [0.8 - 0.6 * math.exp(-0.3 * l) for l in range(depth)]
    diff_scalars = jnp.asarray([_alibi_slopes(DIFF_HEADS) + [li] for li in lam_inits], F32)

    w_in_b = w_in[0].astype(BF16)
    later_ws = (w_branch_swa, w_branch_diff, w_out, w_up, w_down)
    n1g, n2g = norm1_g.reshape(depth, 1, d), norm2_g.reshape(depth, 1, d)
    subln_g = diff_subln_g.reshape(depth, 1, DIFF_VDIM)
    conv_b3 = conv_b.reshape(depth, 1, -1)

    for l in range(depth):
        (qa, kva, qd, kd1, kd2, vd, sg), (w_bs, w_bd, w_o, w_up_b, w_down_b) = _in_proj(
            x, mod, n1g, w_in_b, key_feat, later_ws, l)
        out_a, out_d = _attention(qd, kd1, kd2, vd, diff_scalars, diff_slopes, diff_mask, diff_lambda, subln_g,
                                  qa, kva, swa_sinks, swa_bias, l)
        x, w_in_b = _mix_ffn(out_a, out_d, sg, x, mod, w_bs, w_bd, w_o, n2g, w_up_b, conv_w, conv_b3, w_down_b,
                             final_g, w_in, l, final=(l == depth - 1))
    return x
```

```python
import functools
import math

import jax
import jax.numpy as jnp
import numpy as np
from jax import lax
from jax.experimental import pallas as pl
from jax.experimental.pallas import tpu as pltpu

F32 = jnp.float32
BF16 = jnp.bfloat16

HEAD_DIM = 64
SWA_HEADS = 8
SWA_KV_HEADS = 2
SWA_GROUP = SWA_HEADS // SWA_KV_HEADS
WINDOW = 128
BLOCK = 128
DIFF_HEADS = 4
DIFF_VDIM = 2 * HEAD_DIM
DIFF_VROWS = DIFF_VDIM + 16
CONV_WIDTH = 3
RMS_EPS = 1e-6
NEG_INF = -1e30
LOG2E = math.log2(math.e)

LANES = 128
VMEM_LIMIT_BYTES = 56 * 1024 * 1024

TM = 512
TM_IN = 1024
SWA_TQ = 512
DIFF_TQ = 512
DIFF_TK = 256
FF_CHUNKS = (768, 768, 768, 512)

NT_DIMS = (((1,), (1,)), ((), ()))


def _params(flags=None):
    return pltpu.CompilerParams(vmem_limit_bytes=VMEM_LIMIT_BYTES, flags=flags)


def _layer_block(arr, l):
    nd = arr.ndim - 1
    return pl.BlockSpec((None,) + arr.shape[1:], lambda *_: (l,) + (0,) * nd, pipeline_mode=pl.Buffered(1))


def _mod_block(mod, l):
    return pl.BlockSpec((None, 1) + mod.shape[2:], lambda bi, i: (l, bi, 0, 0))


def _alibi_slopes(n):
    return [2.0 ** (-8.0 * (i + 1) / n) for i in range(n)]


def _ada_kernel(c_ref, w_ref, b_ref, o_ref):
    c = c_ref[...]
    cs = c * jax.nn.sigmoid(c)
    o_ref[0] = jnp.dot(cs.astype(BF16), w_ref[0].astype(BF16), preferred_element_type=F32) + b_ref[0]


def _ada(c8, w_ada, b_ada):
    depth, d, six_d = w_ada.shape
    tn = 1024
    return pl.pallas_call(
        _ada_kernel,
        grid=(depth, six_d // tn),
        in_specs=[
            pl.BlockSpec((8, d), lambda l, j: (0, 0)),
            pl.BlockSpec((1, d, tn), lambda l, j: (l, 0, j)),
            pl.BlockSpec((1, 1, tn), lambda l, j: (l, 0, j)),
        ],
        out_specs=pl.BlockSpec((1, 8, tn), lambda l, j: (l, 0, j)),
        out_shape=jax.ShapeDtypeStruct((depth, 8, six_d), F32),
        compiler_params=_params(),
        name="ada_mod",
    )(c8, w_ada, b_ada.reshape(depth, 1, six_d))


IN_GROUPS = (("qa", 512), ("kva", 256), ("qd", 512), ("kd", 512), ("vd", 512), ("g", 2048))
IN_OFFSETS = {name: sum(w for _, w in IN_GROUPS[:j]) for j, (name, _) in enumerate(IN_GROUPS)}
IN_ORDER = ("g", "vd", "qd", "kd", "qa", "kva")
DIFF_FEATS = 3


N_CAST = 5


def _in_proj_kernel(x_ref, mod_ref, g_ref, w_ref, feat_ref, *refs):
    cast_in, refs = refs[:N_CAST], refs[N_CAST:]
    (qa_ref, kva_ref, qd_ref, kd1_ref, kd2_ref, vd_ref, sg_ref), cast_out = refs[:-N_CAST], refs[-N_CAST:]
    for src, dst in zip(cast_in, cast_out):
        dst[...] = src[...].astype(BF16)
    x = x_ref[0]
    ms = jnp.mean(x * x, axis=-1, keepdims=True)
    y = x * lax.rsqrt(ms + RMS_EPS) * g_ref[...]
    h = y * (1.0 + mod_ref[0, 1:2, :]) + mod_ref[0, 0:1, :]
    hb = h.astype(BF16)
    widths = dict(IN_GROUPS)
    for name in IN_ORDER:
        c0 = IN_OFFSETS[name]
        p = jnp.dot(hb, w_ref[:, c0:c0 + widths[name]], preferred_element_type=F32)
        if name == "qa":
            qa_ref[0] = (p * (HEAD_DIM ** -0.5 * LOG2E)).astype(BF16)
        elif name == "kva":
            kva_ref[0] = p.astype(BF16)
        elif name == "qd":
            qd_ref[0] = (p * (HEAD_DIM ** -0.5 * LOG2E)).T.astype(BF16)
        elif name == "kd":
            lane = lax.broadcasted_iota(jnp.int32, p.shape, 1)
            first_half = (lane & (2 * HEAD_DIM - 1)) < HEAD_DIM
            kd1_ref[0] = jnp.where(first_half, p, feat_ref[0]).astype(BF16)
            kd2_ref[0] = jnp.where(first_half, feat_ref[1], p).astype(BF16)
        elif name == "vd":
            pt = p.T
            ones = jnp.ones((DIFF_VROWS - DIFF_VDIM, DIFF_TK), BF16)
            for hd in range(DIFF_HEADS):
                for t in range(TM_IN // DIFF_TK):
                    vd_ref[0, hd, t, :DIFF_VDIM] = pt[hd * DIFF_VDIM:(hd + 1) * DIFF_VDIM,
                                                      t * DIFF_TK:(t + 1) * DIFF_TK].astype(BF16)
                    vd_ref[0, hd, t, DIFF_VDIM:] = ones
        else:
            sg_ref[0] = jax.nn.sigmoid(p).astype(BF16)


def _key_feature_table():
    r = (np.arange(TM_IN) % DIFF_TK).astype(np.float32)[:, None]
    lane = np.arange(DIFF_HEADS * 2 * HEAD_DIM)[None, :] % (2 * HEAD_DIM)
    f1 = np.where((lane >= HEAD_DIM) & (lane < HEAD_DIM + DIFF_FEATS), r, 0.0)
    f2 = np.where(lane < DIFF_FEATS, r, 0.0)
    return jnp.asarray(np.stack([f1, f2]).astype(np.float32))


def _in_proj(x, mod, norm_g, w_in, feat, cast_ws, l):
    b, s, d = x.shape
    assert len(cast_ws) == N_CAST
    steps_per_b = s // TM_IN
    n_steps = b * steps_per_b
    cast_in_specs = [pl.BlockSpec((None, w.shape[1] // n_steps, w.shape[2]),
                                  lambda bi, i: (l, bi * steps_per_b + i, 0)) for w in cast_ws]
    cast_out = [(pl.BlockSpec((w.shape[1] // n_steps, w.shape[2]), lambda bi, i: (bi * steps_per_b + i, 0)),
                 jax.ShapeDtypeStruct(w.shape[1:], BF16)) for w in cast_ws]
    row = lambda bi, i: (bi, i, 0)
    dq = DIFF_HEADS * 2 * HEAD_DIM
    plain = lambda w: (pl.BlockSpec((1, TM_IN, w), row), jax.ShapeDtypeStruct((b, s, w), BF16))
    outs = [
        plain(512),
        plain(256),
        (pl.BlockSpec((1, dq, TM_IN), lambda bi, i: (bi, 0, i)),
         jax.ShapeDtypeStruct((b, dq, s), BF16)),
        plain(dq),
        plain(dq),
        (pl.BlockSpec((1, DIFF_HEADS, TM_IN // DIFF_TK, DIFF_VROWS, DIFF_TK),
                      lambda bi, i: (bi, 0, i, 0, 0)),
         jax.ShapeDtypeStruct((b, DIFF_HEADS, s // DIFF_TK, DIFF_VROWS, DIFF_TK), BF16)),
        plain(2 * d),
    ]
    res = pl.pallas_call(
        _in_proj_kernel,
        grid=(b, s // TM_IN),
        in_specs=[
            pl.BlockSpec((1, TM_IN, d), row),
            _mod_block(mod, l),
            _layer_block(norm_g, l),
            pl.BlockSpec(w_in.shape, lambda bi, i: (0, 0), pipeline_mode=pl.Buffered(1)),
            pl.BlockSpec(feat.shape, lambda bi, i: (0, 0, 0), pipeline_mode=pl.Buffered(1)),
        ] + cast_in_specs,
        out_specs=[o[0] for o in outs + cast_out],
        out_shape=[o[1] for o in outs + cast_out],
        compiler_params=_params(),
        name="in_proj",
    )(x, mod, norm_g, w_in, feat, *cast_ws)
    return res[:len(outs)], res[len(outs):]


def _swa_prepare(kc_ref, kp_ref, vc_ref, vp_ref, kcat, vcat):
    pr = lax.broadcasted_iota(jnp.int32, (LANES, LANES), 0)
    pc = lax.broadcasted_iota(jnp.int32, (LANES, LANES), 1)
    swap = jnp.where(pc == (pr + HEAD_DIM) % LANES, 1.0, 0.0).astype(BF16)
    low = lax.broadcasted_iota(jnp.int32, (SWA_TQ + BLOCK, LANES), 1) < HEAD_DIM
    k = jnp.concatenate([kp_ref[0], kc_ref[0]], axis=0)
    v = jnp.concatenate([vp_ref[0], vc_ref[0]], axis=0)
    ks = jnp.dot(k, swap, preferred_element_type=F32).astype(BF16)
    vs = jnp.dot(v, swap, preferred_element_type=F32).astype(BF16)
    zero = jnp.zeros_like(k)
    kcat[:, 0 * LANES:1 * LANES] = jnp.where(low, k, zero)
    kcat[:, 1 * LANES:2 * LANES] = jnp.where(low, zero, ks)
    kcat[:, 2 * LANES:3 * LANES] = jnp.where(low, ks, zero)
    kcat[:, 3 * LANES:4 * LANES] = jnp.where(low, zero, k)
    vcat[:, 0 * LANES:1 * LANES] = jnp.where(low, v, vs)
    vcat[:, 1 * LANES:2 * LANES] = jnp.where(low, vs, v)


def _swa_block(sb, layer, sink_ref, q_ref, bias_ref, o_ref, kcat, vcat):
    first = pl.program_id(1) == 0
    col = lax.broadcasted_iota(jnp.int32, (BLOCK, 2 * BLOCK), 1)
    lane = lax.broadcasted_iota(jnp.int32, (BLOCK, LANES), 1)
    r0 = sb * BLOCK
    for t in range(SWA_HEADS // 2):
        kh = t // (SWA_GROUP // 2)
        q_t = q_ref[0, r0:r0 + BLOCK, t * LANES:(t + 1) * LANES]
        v = vcat[r0:r0 + 2 * BLOCK, kh * LANES:(kh + 1) * LANES]
        halves = []
        for e in range(2):
            hq = 2 * t + e
            kx = kcat[r0:r0 + 2 * BLOCK, (2 * kh + e) * LANES:(2 * kh + e + 1) * LANES]
            s = lax.dot_general(q_t, kx, NT_DIMS, preferred_element_type=F32)
            s = s + bias_ref[hq]
            if sb == 0:
                s = jnp.where(first & (col < BLOCK), NEG_INF, s)
            sink = sink_ref[layer, hq] * LOG2E
            m = jnp.maximum(jnp.max(s, axis=-1, keepdims=True), sink)
            p = jnp.exp2(s - m)
            denom = jnp.sum(p, axis=-1, keepdims=True) + jnp.exp2(sink - m)
            o = jnp.dot(p.astype(BF16), v, preferred_element_type=F32)
            halves.append(o / denom)
        o_t = jnp.where(lane < HEAD_DIM, halves[0], halves[1])
        o_ref[0, r0:r0 + BLOCK, t * LANES:(t + 1) * LANES] = o_t.astype(BF16)


def _swa_bias_table():
    slopes = _alibi_slopes(SWA_HEADS)
    qi = np.arange(BLOCK)[:, None]
    kj = np.arange(2 * BLOCK)[None, :]
    delta = qi + BLOCK - kj
    valid = (delta >= 0) & (delta < WINDOW)
    tab = np.stack([np.where(valid, -sl * LOG2E * delta.astype(np.float64), NEG_INF) for sl in slopes])
    return jnp.asarray(tab.astype(np.float32))


def _attn_kernel(sc_ref, sink_ref, q_ref, k1_ref, k2_ref, v_ref, slope_ref, mask_ref, lam_ref, g_ref,
                 qa_ref, kc_ref, kp_ref, vc_ref, vp_ref, swa_bias_ref, o_ref, oa_ref, kcat, vcat, *scratch, layer):
    _swa_prepare(kc_ref, kp_ref, vc_ref, vp_ref, kcat, vcat)
    lam_init = sc_ref[layer, DIFF_HEADS]
    lp = lam_ref[...]
    lam = (jnp.exp(jnp.sum(lp[0:1] * lp[1:2], axis=-1, keepdims=True))
           - jnp.exp(jnp.sum(lp[2:3] * lp[3:4], axis=-1, keepdims=True)) + lam_init)
    per_set = len(scratch) // 2
    for h0 in range(0, DIFF_HEADS, 2):
        _diff_heads([(hd, sc_ref[layer, hd] * LOG2E, scratch[(hd % 2) * per_set:(hd % 2 + 1) * per_set])
                     for hd in (h0, h0 + 1)],
                    lam, lam_init, q_ref, k1_ref, k2_ref, v_ref, slope_ref, mask_ref, g_ref, o_ref)
        for hd in (h0, h0 + 1):
            _swa_block(hd, layer, sink_ref, qa_ref, swa_bias_ref, oa_ref, kcat, vcat)


def _diff_heads(heads, lam, lam_init, q_ref, k1_ref, k2_ref, v_ref, slope_ref, mask_ref, g_ref, o_ref):
    i = pl.program_id(1)
    tiles_per_q = DIFF_TQ // DIFF_TK
    first_diag = tiles_per_q * i
    n_tiles = first_diag + tiles_per_q

    def tile_at(n):
        return jnp.where(n < tiles_per_q, first_diag + n, n_tiles - 1 - n)

    def make(hd, slope2, sa1, sa2, sb1, sb2, acc1, acc2):
        hcols = slice(hd * 2 * HEAD_DIM, (hd + 1) * 2 * HEAD_DIM)
        qt = q_ref[0, hcols, :]
        qrow = lax.broadcasted_iota(jnp.int32, qt.shape, 0)
        q1 = jnp.where(qrow < HEAD_DIM, qt, slope_ref[hd, 0])
        q2 = jnp.where(qrow < HEAD_DIM, slope_ref[hd, 1], qt)

        def stage_a(t, s1, s2, diag=None):
            k0 = pl.multiple_of(t * DIFF_TK, DIFF_TK)
            x1 = jnp.dot(k1_ref[0, pl.ds(k0, DIFF_TK), hcols], q1, preferred_element_type=F32)
            x2 = jnp.dot(k2_ref[0, pl.ds(k0, DIFF_TK), hcols], q2, preferred_element_type=F32)
            if diag is not None:
                x1 = x1 + mask_ref[diag]
                x2 = x2 + mask_ref[diag]
            s1[...] = x1
            s2[...] = x2
            return jnp.max(x1, axis=0, keepdims=True), jnp.max(x2, axis=0, keepdims=True)

        def stage_b(t, mt, s, acc, m_old):
            cj = -slope2 * ((first_diag - t) * DIFF_TK).astype(F32)
            m_new = jnp.maximum(m_old, mt + cj)
            p = jnp.exp2(s[...] - (m_new - cj)).astype(BF16)
            vt = v_ref[0, hd, t]
            acc[...] = jnp.exp2(m_old - m_new) * acc[...] + jnp.dot(vt, p, preferred_element_type=F32)
            return m_new

        def prologue():
            acc1[...] = jnp.zeros_like(acc1)
            acc2[...] = jnp.zeros_like(acc2)
            m0 = jnp.full((1, DIFF_TQ), NEG_INF, F32)
            mta1, mta2 = stage_a(first_diag, sa1, sa2, diag=0)
            mtb1, mtb2 = stage_a(first_diag + 1, sb1, sb2, diag=1)
            return (stage_b(first_diag, mta1, sa1, acc1, m0), stage_b(first_diag, mta2, sa2, acc2, m0),
                    mtb1, mtb2)

        def body(g, carry):
            m1, m2, mtb1, mtb2 = carry
            n = 2 * g
            mta1, mta2 = stage_a(tile_at(n), sa1, sa2)
            m1 = stage_b(tile_at(n - 1), mtb1, sb1, acc1, m1)
            m2 = stage_b(tile_at(n - 1), mtb2, sb2, acc2, m2)
            mtb1, mtb2 = stage_a(tile_at(n + 1), sb1, sb2)
            m1 = stage_b(tile_at(n), mta1, sa1, acc1, m1)
            m2 = stage_b(tile_at(n), mta2, sa2, acc2, m2)
            return m1, m2, mtb1, mtb2

        def epilogue(carry):
            m1, m2, mtb1, mtb2 = carry
            stage_b(tile_at(n_tiles - 1), mtb1, sb1, acc1, m1)
            stage_b(tile_at(n_tiles - 1), mtb2, sb2, acc2, m2)
            l1 = acc1[DIFF_VDIM:DIFF_VDIM + 1, :]
            l2 = acc2[DIFF_VDIM:DIFF_VDIM + 1, :]
            ot = acc1[:DIFF_VDIM, :] / l1 - lam * (acc2[:DIFF_VDIM, :] / l2)
            ms = jnp.mean(ot * ot, axis=0, keepdims=True)
            yt = ot * lax.rsqrt(ms + RMS_EPS)
            o_ref[0, :, hcols] = (yt.T * g_ref[...] * (1.0 - lam_init)).astype(BF16)

        return prologue, body, epilogue

    stages = [make(hd, slope2, *scr) for hd, slope2, scr in heads]
    carry = tuple(prologue() for prologue, _, _ in stages)

    def body_all(g, carry):
        return tuple(body(g, c) for (_, body, _), c in zip(stages, carry))

    n_pairs = n_tiles // 2 - 1
    odd = n_pairs % 2
    carry = lax.fori_loop(1, 1 + odd, body_all, carry)
    carry = lax.fori_loop(
        0, n_pairs // 2, lambda j, c: body_all(2 + odd + 2 * j, body_all(1 + odd + 2 * j, c)), carry)
    for (_, _, epilogue), c in zip(stages, carry):
        epilogue(c)


def _diff_slope_table():
    tab = np.zeros((DIFF_HEADS, 2, 2 * HEAD_DIM, DIFF_TQ), np.float32)
    for hd, sl in enumerate(_alibi_slopes(DIFF_HEADS)):
        rest = np.float32(sl * LOG2E)
        for j in range(DIFF_FEATS):
            piece = np.float32(rest.astype(jnp.bfloat16))
            tab[hd, 0, HEAD_DIM + j, :] = piece
            tab[hd, 1, j, :] = piece
            rest = np.float32(rest - piece)
    return jnp.asarray(tab.astype(jnp.bfloat16))


def _diff_mask_table():
    kr = np.arange(DIFF_TK)[:, None]
    qc = np.arange(DIFF_TQ)[None, :]
    return jnp.asarray(np.stack([np.where(kr + d * DIFF_TK <= qc, 0.0, NEG_INF)
                                 for d in range(DIFF_TQ // DIFF_TK)]).astype(np.float32))


def _attention(qdt, kd1, kd2, vdt, scalars, slopes, mask, lam_p, subln_g, qa, kva, sinks, swa_bias, l):
    b, _, s = qdt.shape
    assert SWA_TQ == DIFF_TQ and SWA_TQ // BLOCK == DIFF_HEADS
    const = lambda arr: pl.BlockSpec(arr.shape, lambda bi, i: (0,) * arr.ndim, pipeline_mode=pl.Buffered(1))
    kspec = pl.BlockSpec((1, s, DIFF_HEADS * 2 * HEAD_DIM), lambda bi, i: (bi, 0, 0))
    row = lambda bi, i: (bi, i, 0)
    prev = lambda i: jnp.maximum(i * (SWA_TQ // BLOCK) - 1, 0)
    out_d, out_a = pl.pallas_call(
        functools.partial(_attn_kernel, layer=l),
        grid=(b, s // DIFF_TQ),
        in_specs=[
            pl.BlockSpec(memory_space=pltpu.SMEM),
            pl.BlockSpec(memory_space=pltpu.SMEM),
            pl.BlockSpec((1, DIFF_HEADS * 2 * HEAD_DIM, DIFF_TQ), lambda bi, i: (bi, 0, i)),
            kspec,
            kspec,
            pl.BlockSpec((1,) + vdt.shape[1:], lambda bi, i: (bi, 0, 0, 0, 0)),
            const(slopes),
            const(mask),
            _layer_block(lam_p, l),
            _layer_block(subln_g, l),
            pl.BlockSpec((1, SWA_TQ, SWA_HEADS * HEAD_DIM), row),
            pl.BlockSpec((1, SWA_TQ, LANES), lambda bi, i: (bi, i, 0)),
            pl.BlockSpec((1, BLOCK, LANES), lambda bi, i: (bi, prev(i), 0)),
            pl.BlockSpec((1, SWA_TQ, LANES), lambda bi, i: (bi, i, 1)),
            pl.BlockSpec((1, BLOCK, LANES), lambda bi, i: (bi, prev(i), 1)),
            const(swa_bias),
        ],
        out_specs=[pl.BlockSpec((1, DIFF_TQ, DIFF_HEADS * DIFF_VDIM), row),
                   pl.BlockSpec((1, SWA_TQ, SWA_HEADS * HEAD_DIM), row)],
        out_shape=[jax.ShapeDtypeStruct((b, s, DIFF_HEADS * DIFF_VDIM), BF16),
                   jax.ShapeDtypeStruct((b, s, SWA_HEADS * HEAD_DIM), BF16)],
        scratch_shapes=([pltpu.VMEM((SWA_TQ + BLOCK, 4 * LANES), BF16), pltpu.VMEM((SWA_TQ + BLOCK, 2 * LANES), BF16)]
                        + ([pltpu.VMEM((DIFF_TK, DIFF_TQ), F32)] * 4
                           + [pltpu.VMEM((DIFF_VROWS, DIFF_TQ), F32)] * 2) * 2),
        compiler_params=_params(),
        name="attention",
    )(scalars, sinks, qdt, kd1, kd2, vdt, slopes, mask, lam_p, subln_g, qa, kva, kva, kva, kva, swa_bias)
    return out_a, out_d


def _mix_ffn_kernel(oa_ref, od_ref, sg_ref, x_ref, mod_ref, wbs_ref, wbd_ref, wo_ref, g_ref, wu_ref, cw_ref,
                    cb_ref, wd_ref, fg_ref, win_ref, o_ref, win_out_ref, yd_acc, carry, *, final):
    win_out_ref[...] = win_ref[...].astype(BF16)
    d = x_ref.shape[-1]
    d_ff = wd_ref.shape[0]
    tm = x_ref.shape[1]

    @pl.when(pl.program_id(1) == 0)
    def _():
        carry[...] = jnp.zeros_like(carry)

    pa = jnp.dot(oa_ref[0], wbs_ref[...], preferred_element_type=F32)
    pd = jnp.dot(od_ref[0], wbd_ref[...], preferred_element_type=F32)
    merged = sg_ref[0, :, :d].astype(F32) * pa + sg_ref[0, :, d:].astype(F32) * pd
    x = x_ref[0] + mod_ref[0, 2:3, :] * jnp.dot(merged.astype(BF16), wo_ref[...], preferred_element_type=F32)

    ms = jnp.mean(x * x, axis=-1, keepdims=True)
    y = x * lax.rsqrt(ms + RMS_EPS) * g_ref[...]
    h = y * (1.0 + mod_ref[0, 4:5, :]) + mod_ref[0, 3:4, :]
    hb = h.astype(BF16)

    c0 = 0
    for width in FF_CHUNKS:
        a = jnp.dot(hb, wu_ref[:, c0:c0 + width], preferred_element_type=F32)
        gate = jnp.dot(hb, wu_ref[:, d_ff + c0:d_ff + c0 + width], preferred_element_type=F32)
        full = jnp.concatenate([carry[:, c0:c0 + width], a], axis=0)
        carry[:, c0:c0 + width] = a[tm - 8:, :]
        a1 = pltpu.roll(full, 1, 0)[8:]
        a2 = pltpu.roll(full, 2, 0)[8:]
        z = (a2 * cw_ref[0:1, c0:c0 + width] + a1 * cw_ref[1:2, c0:c0 + width]
             + a * cw_ref[2:3, c0:c0 + width] + cb_ref[:, c0:c0 + width])
        gelu = 0.5 * z * (1.0 + lax.erf(z * (2.0 ** -0.5)))
        part = jnp.dot((gelu * gate).astype(BF16), wd_ref[c0:c0 + width, :], preferred_element_type=F32)
        if c0 == 0:
            yd_acc[...] = part
        else:
            yd_acc[...] += part
        c0 += width

    out = x + mod_ref[0, 5:6, :] * yd_acc[...]
    if final:
        ms2 = jnp.mean(out * out, axis=-1, keepdims=True)
        out = out * lax.rsqrt(ms2 + RMS_EPS) * fg_ref[...]
    o_ref[0] = out


def _mix_ffn(out_a, out_d, sg, x, mod, w_bs, w_bd, w_o, norm_g, w_up, conv_w, conv_b, w_down, final_g, w_in,
             l, final):
    b, s, d = x.shape
    d_ff = w_down.shape[0]
    depth, win_rows, win_cols = w_in.shape
    steps_per_b = s // TM
    chunk = win_rows // (b * steps_per_b)
    row = lambda bi, i: (bi, i, 0)
    whole = lambda w: pl.BlockSpec(w.shape, lambda bi, i: (0, 0), pipeline_mode=pl.Buffered(1))
    return pl.pallas_call(
        functools.partial(_mix_ffn_kernel, final=final),
        grid=(b, s // TM),
        in_specs=[
            pl.BlockSpec((1, TM, out_a.shape[-1]), row),
            pl.BlockSpec((1, TM, out_d.shape[-1]), row),
            pl.BlockSpec((1, TM, 2 * d), row),
            pl.BlockSpec((1, TM, d), row),
            _mod_block(mod, l),
            whole(w_bs),
            whole(w_bd),
            whole(w_o),
            _layer_block(norm_g, l),
            whole(w_up),
            _layer_block(conv_w, l),
            _layer_block(conv_b, l),
            whole(w_down),
            pl.BlockSpec((1, d), lambda bi, i: (0, 0)),
            pl.BlockSpec((None, chunk, win_cols), lambda bi, i: ((l + 1) % depth, bi * steps_per_b + i, 0)),
        ],
        out_specs=[pl.BlockSpec((1, TM, d), row),
                   pl.BlockSpec((chunk, win_cols), lambda bi, i: (bi * steps_per_b + i, 0))],
        out_shape=[jax.ShapeDtypeStruct((b, s, d), F32), jax.ShapeDtypeStruct((win_rows, win_cols), BF16)],
        scratch_shapes=[pltpu.VMEM((TM, d), F32), pltpu.VMEM((8, d_ff), F32)],
        compiler_params=_params(),
        name="mix_ffn",
    )(out_a, out_d, sg, x, mod, w_bs, w_bd, w_o, norm_g, w_up, conv_w, conv_b, w_down, final_g.reshape(1, d), w_in)


def kernel(x, c, w_ada, b_ada, norm1_g, w_in, swa_sinks, diff_lambda, diff_subln_g, w_branch_swa,
           w_branch_diff, w_out, norm2_g, w_up, conv_w, conv_b, w_down, final_g):
    b, s, d = x.shape
    depth = w_ada.shape[0]

    c8 = jnp.zeros((8, d), F32).at[:b].set(c)
    mod = _ada(c8, w_ada, b_ada).reshape(depth, 8, 6, d)

    swa_bias = _swa_bias_table()
    diff_slopes = _diff_slope_table()
    diff_mask = _diff_mask_table()
    key_feat = _key_feature_table()
    lam_inits = [0.8 - 0.6 * math.exp(-0.3 * l) for l in range(depth)]
    diff_scalars = jnp.asarray([_alibi_slopes(DIFF_HEADS) + [li] for li in lam_inits], F32)

    w_in_b = w_in[0].astype(BF16)
    later_ws = (w_branch_swa, w_branch_diff, w_out, w_up, w_down)
    n1g, n2g = norm1_g.reshape(depth, 1, d), norm2_g.reshape(depth, 1, d)
    subln_g = diff_subln_g.reshape(depth, 1, DIFF_VDIM)
    conv_b3 = conv_b.reshape(depth, 1, -1)

    for l in range(depth):
        (qa, kva, qd, kd1, kd2, vd, sg), (w_bs, w_bd, w_o, w_up_b, w_down_b) = _in_proj(
            x, mod, n1g, w_in_b, key_feat, later_ws, l)
        out_a, out_d = _attention(qd, kd1, kd2, vd, diff_scalars, diff_slopes, diff_mask, diff_lambda, subln_g,
                                  qa, kva, swa_sinks, swa_bias, l)
        x, w_in_b = _mix_ffn(out_a, out_d, sg, x, mod, w_bs, w_bd, w_o, n2g, w_up_b, conv_w, conv_b3, w_down_b,
                             final_g, w_in, l, final=(l == depth - 1))
    return x
```

```python
import functools
import math

import jax
import jax.numpy as jnp
import numpy as np
from jax import lax
from jax.experimental import pallas as pl
from jax.experimental.pallas import tpu as pltpu

F32 = jnp.float32
BF16 = jnp.bfloat16

HEAD_DIM = 64
SWA_HEADS = 8
SWA_KV_HEADS = 2
SWA_GROUP = SWA_HEADS // SWA_KV_HEADS
WINDOW = 128
BLOCK = 128
DIFF_HEADS = 4
DIFF_VDIM = 2 * HEAD_DIM
DIFF_VROWS = DIFF_VDIM + 16
CONV_WIDTH = 3
RMS_EPS = 1e-6
NEG_INF = -1e30
LOG2E = math.log2(math.e)

LANES = 128
VMEM_LIMIT_BYTES = 56 * 1024 * 1024

TM = 512
TM_IN = 1024
SWA_TQ = 512
DIFF_TQ = 512
DIFF_TK = 256
FF_CHUNKS = (768, 768, 768, 512)

NT_DIMS = (((1,), (1,)), ((), ()))


def _params(flags=None):
    return pltpu.CompilerParams(vmem_limit_bytes=VMEM_LIMIT_BYTES, flags=flags)


def _layer_block(arr, l):
    nd = arr.ndim - 1
    return pl.BlockSpec((None,) + arr.shape[1:], lambda *_: (l,) + (0,) * nd, pipeline_mode=pl.Buffered(1))


def _mod_block(mod, l):
    return pl.BlockSpec((None, 1) + mod.shape[2:], lambda bi, i: (l, bi, 0, 0))


def _alibi_slopes(n):
    return [2.0 ** (-8.0 * (i + 1) / n) for i in range(n)]


def _ada_kernel(c_ref, w_ref, b_ref, o_ref):
    c = c_ref[...]
    cs = c * jax.nn.sigmoid(c)
    o_ref[0] = jnp.dot(cs.astype(BF16), w_ref[0].astype(BF16), preferred_element_type=F32) + b_ref[0]


def _ada(c8, w_ada, b_ada):
    depth, d, six_d = w_ada.shape
    tn = 1024
    return pl.pallas_call(
        _ada_kernel,
        grid=(depth, six_d // tn),
        in_specs=[
            pl.BlockSpec((8, d), lambda l, j: (0, 0)),
            pl.BlockSpec((1, d, tn), lambda l, j: (l, 0, j)),
            pl.BlockSpec((1, 1, tn), lambda l, j: (l, 0, j)),
        ],
        out_specs=pl.BlockSpec((1, 8, tn), lambda l, j: (l, 0, j)),
        out_shape=jax.ShapeDtypeStruct((depth, 8, six_d), F32),
        compiler_params=_params(),
        name="ada_mod",
    )(c8, w_ada, b_ada.reshape(depth, 1, six_d))


IN_GROUPS = (("qa", 512), ("kva", 256), ("qd", 512), ("kd", 512), ("vd", 512), ("g", 2048))
IN_OFFSETS = {name: sum(w for _, w in IN_GROUPS[:j]) for j, (name, _) in enumerate(IN_GROUPS)}
IN_ORDER = ("g", "vd", "qd", "kd", "qa", "kva")
DIFF_FEATS = 3


N_CAST = 5


def _in_proj_kernel(x_ref, mod_ref, g_ref, w_ref, feat_ref, *refs):
    cast_in, refs = refs[:N_CAST], refs[N_CAST:]
    (qa_ref, kva_ref, qd_ref, kd1_ref, kd2_ref, vd_ref, sg_ref), cast_out = refs[:-N_CAST], refs[-N_CAST:]
    for src, dst in zip(cast_in, cast_out):
        dst[...] = src[...].astype(BF16)
    x = x_ref[0]
    ms = jnp.mean(x * x, axis=-1, keepdims=True)
    y = x * lax.rsqrt(ms + RMS_EPS) * g_ref[...]
    h = y * (1.0 + mod_ref[0, 1:2, :]) + mod_ref[0, 0:1, :]
    hb = h.astype(BF16)
    widths = dict(IN_GROUPS)
    for name in IN_ORDER:
        c0 = IN_OFFSETS[name]
        p = jnp.dot(hb, w_ref[:, c0:c0 + widths[name]], preferred_element_type=F32)
        if name == "qa":
            qa_ref[0] = (p * (HEAD_DIM ** -0.5 * LOG2E)).astype(BF16)
        elif name == "kva":
            kva_ref[0] = p.astype(BF16)
        elif name == "qd":
            qd_ref[0] = (p * (HEAD_DIM ** -0.5 * LOG2E)).T.astype(BF16)
        elif name == "kd":
            lane = lax.broadcasted_iota(jnp.int32, p.shape, 1)
            first_half = (lane & (2 * HEAD_DIM - 1)) < HEAD_DIM
            kd1_ref[0] = jnp.where(first_half, p, feat_ref[0]).astype(BF16)
            kd2_ref[0] = jnp.where(first_half, feat_ref[1], p).astype(BF16)
        elif name == "vd":
            pt = p.T
            ones = jnp.ones((DIFF_VROWS - DIFF_VDIM, DIFF_TK), BF16)
            for hd in range(DIFF_HEADS):
                for t in range(TM_IN // DIFF_TK):
                    vd_ref[0, hd, t, :DIFF_VDIM] = pt[hd * DIFF_VDIM:(hd + 1) * DIFF_VDIM,
                                                      t * DIFF_TK:(t + 1) * DIFF_TK].astype(BF16)
                    vd_ref[0, hd, t, DIFF_VDIM:] = ones
        else:
            sg_ref[0] = jax.nn.sigmoid(p).astype(BF16)


def _key_feature_table():
    r = (np.arange(TM_IN) % DIFF_TK).astype(np.float32)[:, None]
    lane = np.arange(DIFF_HEADS * 2 * HEAD_DIM)[None, :] % (2 * HEAD_DIM)
    f1 = np.where((lane >= HEAD_DIM) & (lane < HEAD_DIM + DIFF_FEATS), r, 0.0)
    f2 = np.where(lane < DIFF_FEATS, r, 0.0)
    return jnp.asarray(np.stack([f1, f2]).astype(np.float32))


def _in_proj(x, mod, norm_g, w_in, feat, cast_ws, l):
    b, s, d = x.shape
    assert len(cast_ws) == N_CAST
    steps_per_b = s // TM_IN
    n_steps = b * steps_per_b
    cast_in_specs = [pl.BlockSpec((None, w.shape[1] // n_steps, w.shape[2]),
                                  lambda bi, i: (l, bi * steps_per_b + i, 0)) for w in cast_ws]
    cast_out = [(pl.BlockSpec((w.shape[1] // n_steps, w.shape[2]), lambda bi, i: (bi * steps_per_b + i, 0)),
                 jax.ShapeDtypeStruct(w.shape[1:], BF16)) for w in cast_ws]
    row = lambda bi, i: (bi, i, 0)
    dq = DIFF_HEADS * 2 * HEAD_DIM
    plain = lambda w: (pl.BlockSpec((1, TM_IN, w), row), jax.ShapeDtypeStruct((b, s, w), BF16))
    outs = [
        plain(512),
        plain(256),
        (pl.BlockSpec((1, dq, TM_IN), lambda bi, i: (bi, 0, i)),
         jax.ShapeDtypeStruct((b, dq, s), BF16)),
        plain(dq),
        plain(dq),
        (pl.BlockSpec((1, DIFF_HEADS, TM_IN // DIFF_TK, DIFF_VROWS, DIFF_TK),
                      lambda bi, i: (bi, 0, i, 0, 0)),
         jax.ShapeDtypeStruct((b, DIFF_HEADS, s // DIFF_TK, DIFF_VROWS, DIFF_TK), BF16)),
        plain(2 * d),
    ]
    res = pl.pallas_call(
        _in_proj_kernel,
        grid=(b, s // TM_IN),
        in_specs=[
            pl.BlockSpec((1, TM_IN, d), row),
            _mod_block(mod, l),
            _layer_block(norm_g, l),
            pl.BlockSpec(w_in.shape, lambda bi, i: (0, 0), pipeline_mode=pl.Buffered(1)),
            pl.BlockSpec(feat.shape, lambda bi, i: (0, 0, 0), pipeline_mode=pl.Buffered(1)),
        ] + cast_in_specs,
        out_specs=[o[0] for o in outs + cast_out],
        out_shape=[o[1] for o in outs + cast_out],
        compiler_params=_params(),
        name="in_proj",
    )(x, mod, norm_g, w_in, feat, *cast_ws)
    return res[:len(outs)], res[len(outs):]


def _swa_prepare(kc_ref, kp_ref, vc_ref, vp_ref, kcat, vcat):
    pr = lax.broadcasted_iota(jnp.int32, (LANES, LANES), 0)
    pc = lax.broadcasted_iota(jnp.int32, (LANES, LANES), 1)
    swap = jnp.where(pc == (pr + HEAD_DIM) % LANES, 1.0, 0.0).astype(BF16)
    low = lax.broadcasted_iota(jnp.int32, (SWA_TQ + BLOCK, LANES), 1) < HEAD_DIM
    k = jnp.concatenate([kp_ref[0], kc_ref[0]], axis=0)
    v = jnp.concatenate([vp_ref[0], vc_ref[0]], axis=0)
    ks = jnp.dot(k, swap, preferred_element_type=F32).astype(BF16)
    vs = jnp.dot(v, swap, preferred_element_type=F32).astype(BF16)
    zero = jnp.zeros_like(k)
    kcat[:, 0 * LANES:1 * LANES] = jnp.where(low, k, zero)
    kcat[:, 1 * LANES:2 * LANES] = jnp.where(low, zero, ks)
    kcat[:, 2 * LANES:3 * LANES] = jnp.where(low, ks, zero)
    kcat[:, 3 * LANES:4 * LANES] = jnp.where(low, zero, k)
    vcat[:, 0 * LANES:1 * LANES] = jnp.where(low, v, vs)
    vcat[:, 1 * LANES:2 * LANES] = jnp.where(low, vs, v)


def _swa_block(sb, layer, sink_ref, q_ref, bias_ref, o_ref, kcat, vcat):
    first = pl.program_id(1) == 0
    col = lax.broadcasted_iota(jnp.int32, (BLOCK, 2 * BLOCK), 1)
    lane = lax.broadcasted_iota(jnp.int32, (BLOCK, LANES), 1)
    r0 = sb * BLOCK
    for t in range(SWA_HEADS // 2):
        kh = t // (SWA_GROUP // 2)
        q_t = q_ref[0, r0:r0 + BLOCK, t * LANES:(t + 1) * LANES]
        v = vcat[r0:r0 + 2 * BLOCK, kh * LANES:(kh + 1) * LANES]
        halves = []
        for e in range(2):
            hq = 2 * t + e
            kx = kcat[r0:r0 + 2 * BLOCK, (2 * kh + e) * LANES:(2 * kh + e + 1) * LANES]
            s = lax.dot_general(q_t, kx, NT_DIMS, preferred_element_type=F32)
            s = s + bias_ref[hq]
            if sb == 0:
                s = jnp.where(first & (col < BLOCK), NEG_INF, s)
            sink = sink_ref[layer, hq] * LOG2E
            m = jnp.maximum(jnp.max(s, axis=-1, keepdims=True), sink)
            p = jnp.exp2(s - m)
            denom = jnp.sum(p, axis=-1, keepdims=True) + jnp.exp2(sink - m)
            o = jnp.dot(p.astype(BF16), v, preferred_element_type=F32)
            halves.append(o / denom)
        o_t = jnp.where(lane < HEAD_DIM, halves[0], halves[1])
        o_ref[0, r0:r0 + BLOCK, t * LANES:(t + 1) * LANES] = o_t.astype(BF16)


def _swa_bias_table():
    slopes = _alibi_slopes(SWA_HEADS)
    qi = np.arange(BLOCK)[:, None]
    kj = np.arange(2 * BLOCK)[None, :]
    delta = qi + BLOCK - kj
    valid = (delta >= 0) & (delta < WINDOW)
    tab = np.stack([np.where(valid, -sl * LOG2E * delta.astype(np.float64), NEG_INF) for sl in slopes])
    return jnp.asarray(tab.astype(np.float32))


def _attn_kernel(sc_ref, sink_ref, q_ref, k1_ref, k2_ref, v_ref, slope_ref, mask_ref, lam_ref, g_ref,
                 qa_ref, kc_ref, kp_ref, vc_ref, vp_ref, swa_bias_ref, o_ref, oa_ref, kcat, vcat, *scratch, layer):
    _swa_prepare(kc_ref, kp_ref, vc_ref, vp_ref, kcat, vcat)
    lam_init = sc_ref[layer, DIFF_HEADS]
    lp = lam_ref[...]
    lam = (jnp.exp(jnp.sum(lp[0:1] * lp[1:2], axis=-1, keepdims=True))
           - jnp.exp(jnp.sum(lp[2:3] * lp[3:4], axis=-1, keepdims=True)) + lam_init)
    per_set = len(scratch) // 2
    for h0 in range(0, DIFF_HEADS, 2):
        _diff_heads([(hd, sc_ref[layer, hd] * LOG2E, scratch[(hd % 2) * per_set:(hd % 2 + 1) * per_set])
                     for hd in (h0, h0 + 1)],
                    lam, lam_init, q_ref, k1_ref, k2_ref, v_ref, slope_ref, mask_ref, g_ref, o_ref)
        for hd in (h0, h0 + 1):
            _swa_block(hd, layer, sink_ref, qa_ref, swa_bias_ref, oa_ref, kcat, vcat)


def _diff_heads(heads, lam, lam_init, q_ref, k1_ref, k2_ref, v_ref, slope_ref, mask_ref, g_ref, o_ref):
    i = pl.program_id(1)
    tiles_per_q = DIFF_TQ // DIFF_TK
    first_diag = tiles_per_q * i
    n_tiles = first_diag + tiles_per_q

    def tile_at(n):
        return jnp.where(n < tiles_per_q, first_diag + n, n_tiles - 1 - n)

    def make(hd, slope2, sa1, sa2, sb1, sb2, acc1, acc2):
        hcols = slice(hd * 2 * HEAD_DIM, (hd + 1) * 2 * HEAD_DIM)
        qt = q_ref[0, hcols, :]
        qrow = lax.broadcasted_iota(jnp.int32, qt.shape, 0)
        q1 = jnp.where(qrow < HEAD_DIM, qt, slope_ref[hd, 0])
        q2 = jnp.where(qrow < HEAD_DIM, slope_ref[hd, 1], qt)

        def stage_a(t, s1, s2, diag=None):
            k0 = pl.multiple_of(t * DIFF_TK, DIFF_TK)
            x1 = jnp.dot(k1_ref[0, pl.ds(k0, DIFF_TK), hcols], q1, preferred_element_type=F32)
            x2 = jnp.dot(k2_ref[0, pl.ds(k0, DIFF_TK), hcols], q2, preferred_element_type=F32)
            if diag is not None:
                x1 = x1 + mask_ref[diag]
                x2 = x2 + mask_ref[diag]
            s1[...] = x1
            s2[...] = x2
            return jnp.max(x1, axis=0, keepdims=True), jnp.max(x2, axis=0, keepdims=True)

        def stage_b(t, mt, s, acc, m_old):
            cj = -slope2 * ((first_diag - t) * DIFF_TK).astype(F32)
            m_new = jnp.maximum(m_old, mt + cj)
            p = jnp.exp2(s[...] - (m_new - cj)).astype(BF16)
            vt = v_ref[0, hd, t]
            acc[...] = jnp.exp2(m_old - m_new) * acc[...] + jnp.dot(vt, p, preferred_element_type=F32)
            return m_new

        def prologue():
            acc1[...] = jnp.zeros_like(acc1)
            acc2[...] = jnp.zeros_like(acc2)
            m0 = jnp.full((1, DIFF_TQ), NEG_INF, F32)
            mta1, mta2 = stage_a(first_diag, sa1, sa2, diag=0)
            mtb1, mtb2 = stage_a(first_diag + 1, sb1, sb2, diag=1)
            return (stage_b(first_diag, mta1, sa1, acc1, m0), stage_b(first_diag, mta2, sa2, acc2, m0),
                    mtb1, mtb2)

        def body(g, carry):
            m1, m2, mtb1, mtb2 = carry
            n = 2 * g
            mta1, mta2 = stage_a(tile_at(n), sa1, sa2)
            m1 = stage_b(tile_at(n - 1), mtb1, sb1, acc1, m1)
            m2 = stage_b(tile_at(n - 1), mtb2, sb2, acc2, m2)
            mtb1, mtb2 = stage_a(tile_at(n + 1), sb1, sb2)
            m1 = stage_b(tile_at(n), mta1, sa1, acc1, m1)
            m2 = stage_b(tile_at(n), mta2, sa2, acc2, m2)
            return m1, m2, mtb1, mtb2

        def epilogue(carry):
            m1, m2, mtb1, mtb2 = carry
            stage_b(tile_at(n_tiles - 1), mtb1, sb1, acc1, m1)
            stage_b(tile_at(n_tiles - 1), mtb2, sb2, acc2, m2)
            l1 = acc1[DIFF_VDIM:DIFF_VDIM + 1, :]
            l2 = acc2[DIFF_VDIM:DIFF_VDIM + 1, :]
            ot = acc1[:DIFF_VDIM, :] / l1 - lam * (acc2[:DIFF_VDIM, :] / l2)
            ms = jnp.mean(ot * ot, axis=0, keepdims=True)
            yt = ot * lax.rsqrt(ms + RMS_EPS)
            o_ref[0, :, hcols] = (yt.T * g_ref[...] * (1.0 - lam_init)).astype(BF16)

        return prologue, body, epilogue

    stages = [make(hd, slope2, *scr) for hd, slope2, scr in heads]
    carry = tuple(prologue() for prologue, _, _ in stages)

    def body_all(g, carry):
        return tuple(body(g, c) for (_, body, _), c in zip(stages, carry))

    n_pairs = n_tiles // 2 - 1
    odd = n_pairs % 2
    carry = lax.fori_loop(1, 1 + odd, body_all, carry)
    carry = lax.fori_loop(
        0, n_pairs // 2, lambda j, c: body_all(2 + odd + 2 * j, body_all(1 + odd + 2 * j, c)), carry)
    for (_, _, epilogue), c in zip(stages, carry):
        epilogue(c)


def _diff_slope_table():
    tab = np.zeros((DIFF_HEADS, 2, 2 * HEAD_DIM, DIFF_TQ), np.float32)
    for hd, sl in enumerate(_alibi_slopes(DIFF_HEADS)):
        rest = np.float32(sl * LOG2E)
        for j in range(DIFF_FEATS):
            piece = np.float32(rest.astype(jnp.bfloat16))
            tab[hd, 0, HEAD_DIM + j, :] = piece
            tab[hd, 1, j, :] = piece
            rest = np.float32(rest - piece)
    return jnp.asarray(tab.astype(jnp.bfloat16))


def _diff_mask_table():
    kr = np.arange(DIFF_TK)[:, None]
    qc = np.arange(DIFF_TQ)[None, :]
    return jnp.asarray(np.stack([np.where(kr + d * DIFF_TK <= qc, 0.0, NEG_INF)
                                 for d in range(DIFF_TQ // DIFF_TK)]).astype(np.float32))


def _attention(qdt, kd1, kd2, vdt, scalars, slopes, mask, lam_p, subln_g, qa, kva, sinks, swa_bias, l):
    b, _, s = qdt.shape
    assert SWA_TQ == DIFF_TQ and SWA_TQ // BLOCK == DIFF_HEADS
    const = lambda arr: pl.BlockSpec(arr.shape, lambda bi, i: (0,) * arr.ndim, pipeline_mode=pl.Buffered(1))
    kspec = pl.BlockSpec((1, s, DIFF_HEADS * 2 * HEAD_DIM), lambda bi, i: (bi, 0, 0))
    row = lambda bi, i: (bi, i, 0)
    prev = lambda i: jnp.maximum(i * (SWA_TQ // BLOCK) - 1, 0)
    out_d, out_a = pl.pallas_call(
        functools.partial(_attn_kernel, layer=l),
        grid=(b, s // DIFF_TQ),
        in_specs=[
            pl.BlockSpec(memory_space=pltpu.SMEM),
            pl.BlockSpec(memory_space=pltpu.SMEM),
            pl.BlockSpec((1, DIFF_HEADS * 2 * HEAD_DIM, DIFF_TQ), lambda bi, i: (bi, 0, i)),
            kspec,
            kspec,
            pl.BlockSpec((1,) + vdt.shape[1:], lambda bi, i: (bi, 0, 0, 0, 0)),
            const(slopes),
            const(mask),
            _layer_block(lam_p, l),
            _layer_block(subln_g, l),
            pl.BlockSpec((1, SWA_TQ, SWA_HEADS * HEAD_DIM), row),
            pl.BlockSpec((1, SWA_TQ, LANES), lambda bi, i: (bi, i, 0)),
            pl.BlockSpec((1, BLOCK, LANES), lambda bi, i: (bi, prev(i), 0)),
            pl.BlockSpec((1, SWA_TQ, LANES), lambda bi, i: (bi, i, 1)),
            pl.BlockSpec((1, BLOCK, LANES), lambda bi, i: (bi, prev(i), 1)),
            const(swa_bias),
        ],
        out_specs=[pl.BlockSpec((1, DIFF_TQ, DIFF_HEADS * DIFF_VDIM), row),
                   pl.BlockSpec((1, SWA_TQ, SWA_HEADS * HEAD_DIM), row)],
        out_shape=[jax.ShapeDtypeStruct((b, s, DIFF_HEADS * DIFF_VDIM), BF16),
                   jax.ShapeDtypeStruct((b, s, SWA_HEADS * HEAD_DIM), BF16)],
        scratch_shapes=([pltpu.VMEM((SWA_TQ + BLOCK, 4 * LANES), BF16), pltpu.VMEM((SWA_TQ + BLOCK, 2 * LANES), BF16)]
                        + ([pltpu.VMEM((DIFF_TK, DIFF_TQ), F32)] * 4
                           + [pltpu.VMEM((DIFF_VROWS, DIFF_TQ), F32)] * 2) * 2),
        compiler_params=_params(),
        name="attention",
    )(scalars, sinks, qdt, kd1, kd2, vdt, slopes, mask, lam_p, subln_g, qa, kva, kva, kva, kva, swa_bias)
    return out_a, out_d


def _mix_ffn_kernel(oa_ref, od_ref, sg_ref, x_ref, mod_ref, wbs_ref, wbd_ref, wo_ref, g_ref, wu_ref, cw_ref,
                    cb_ref, wd_ref, fg_ref, win_ref, o_ref, win_out_ref, act, carry, *, final):
    win_out_ref[...] = win_ref[...].astype(BF16)
    d = x_ref.shape[-1]
    d_ff = wd_ref.shape[0]
    tm = x_ref.shape[1]

    @pl.when(pl.program_id(1) == 0)
    def _():
        carry[...] = jnp.zeros_like(carry)

    xs, hbs = [], []
    for r0 in (0, tm // 2):
        rows = slice(r0, r0 + tm // 2)
        pa = jnp.dot(oa_ref[0, rows, :], wbs_ref[...], preferred_element_type=F32)
        pd = jnp.dot(od_ref[0, rows, :], wbd_ref[...], preferred_element_type=F32)
        merged = sg_ref[0, rows, :d].astype(F32) * pa + sg_ref[0, rows, d:].astype(F32) * pd
        xh = x_ref[0, rows, :] + mod_ref[0, 2:3, :] * jnp.dot(merged.astype(BF16), wo_ref[...],
                                                              preferred_element_type=F32)
        ms = jnp.mean(xh * xh, axis=-1, keepdims=True)
        y = xh * lax.rsqrt(ms + RMS_EPS) * g_ref[...]
        h = y * (1.0 + mod_ref[0, 4:5, :]) + mod_ref[0, 3:4, :]
        xs.append(xh)
        hbs.append(h.astype(BF16))
    x = jnp.concatenate(xs, axis=0)
    hb = jnp.concatenate(hbs, axis=0)

    c0 = 0
    for width in FF_CHUNKS:
        a = jnp.dot(hb, wu_ref[:, c0:c0 + width], preferred_element_type=F32)
        gate = jnp.dot(hb, wu_ref[:, d_ff + c0:d_ff + c0 + width], preferred_element_type=F32)
        full = jnp.concatenate([carry[:, c0:c0 + width], a], axis=0)
        carry[:, c0:c0 + width] = a[tm - 8:, :]
        a1 = pltpu.roll(full, 1, 0)[8:]
        a2 = pltpu.roll(full, 2, 0)[8:]
        z = (a2 * cw_ref[0:1, c0:c0 + width] + a1 * cw_ref[1:2, c0:c0 + width]
             + a * cw_ref[2:3, c0:c0 + width] + cb_ref[:, c0:c0 + width])
        gelu = 0.5 * z * (1.0 + lax.erf(z * (2.0 ** -0.5)))
        act[:, c0:c0 + width] = (gelu * gate).astype(BF16)
        c0 += width

    yd = jnp.dot(act[...], wd_ref[...], preferred_element_type=F32)
    out = x + mod_ref[0, 5:6, :] * yd
    if final:
        ms2 = jnp.mean(out * out, axis=-1, keepdims=True)
        out = out * lax.rsqrt(ms2 + RMS_EPS) * fg_ref[...]
    o_ref[0] = out


def _mix_ffn(out_a, out_d, sg, x, mod, w_bs, w_bd, w_o, norm_g, w_up, conv_w, conv_b, w_down, final_g, w_in,
             l, final):
    b, s, d = x.shape
    d_ff = w_down.shape[0]
    depth, win_rows, win_cols = w_in.shape
    steps_per_b = s // TM
    chunk = win_rows // (b * steps_per_b)
    row = lambda bi, i: (bi, i, 0)
    whole = lambda w: pl.BlockSpec(w.shape, lambda bi, i: (0, 0), pipeline_mode=pl.Buffered(1))
    return pl.pallas_call(
        functools.partial(_mix_ffn_kernel, final=final),
        grid=(b, s // TM),
        in_specs=[
            pl.BlockSpec((1, TM, out_a.shape[-1]), row),
            pl.BlockSpec((1, TM, out_d.shape[-1]), row),
            pl.BlockSpec((1, TM, 2 * d), row),
            pl.BlockSpec((1, TM, d), row),
            _mod_block(mod, l),
            whole(w_bs),
            whole(w_bd),
            whole(w_o),
            _layer_block(norm_g, l),
            whole(w_up),
            _layer_block(conv_w, l),
            _layer_block(conv_b, l),
            whole(w_down),
            pl.BlockSpec((1, d), lambda bi, i: (0, 0)),
            pl.BlockSpec((None, chunk, win_cols), lambda bi, i: ((l + 1) % depth, bi * steps_per_b + i, 0)),
        ],
        out_specs=[pl.BlockSpec((1, TM, d), row),
                   pl.BlockSpec((chunk, win_cols), lambda bi, i: (bi * steps_per_b + i, 0))],
        out_shape=[jax.ShapeDtypeStruct((b, s, d), F32), jax.ShapeDtypeStruct((win_rows, win_cols), BF16)],
        scratch_shapes=[pltpu.VMEM((TM, d_ff), BF16), pltpu.VMEM((8, d_ff), F32)],
        compiler_params=_params(),
        name="mix_ffn",
    )(out_a, out_d, sg, x, mod, w_bs, w_bd, w_o, norm_g, w_up, conv_w, conv_b, w_down, final_g.reshape(1, d), w_in)


def kernel(x, c, w_ada, b_ada, norm1_g, w_in, swa_sinks, diff_lambda, diff_subln_g, w_branch_swa,
           w_branch_diff, w_out, norm2_g, w_up, conv_w, conv_b, w_down, final_g):
    b, s, d = x.shape
    depth = w_ada.shape[0]

    c8 = jnp.zeros((8, d), F32).at[:b].set(c)
    mod = _ada(c8, w_ada, b_ada).reshape(depth, 8, 6, d)

    swa_bias = _swa_bias_table()
    diff_slopes = _diff_slope_table()
    diff_mask = _diff_mask_table()
    key_feat = _key_feature_table()
    lam_inits = [0.8 - 0.6 * math.exp(-0.3 * l) for l in range(depth)]
    diff_scalars = jnp.asarray([_alibi_slopes(DIFF_HEADS) + [li] for li in lam_inits], F32)

    w_in_b = w_in[0].astype(BF16)
    later_ws = (w_branch_swa, w_branch_diff, w_out, w_up, w_down)
    n1g, n2g = norm1_g.reshape(depth, 1, d), norm2_g.reshape(depth, 1, d)
    subln_g = diff_subln_g.reshape(depth, 1, DIFF_VDIM)
    conv_b3 = conv_b.reshape(depth, 1, -1)

    for l in range(depth):
        (qa, kva, qd, kd1, kd2, vd, sg), (w_bs, w_bd, w_o, w_up_b, w_down_b) = _in_proj(
            x, mod, n1g, w_in_b, key_feat, later_ws, l)
        out_a, out_d = _attention(qd, kd1, kd2, vd, diff_scalars, diff_slopes, diff_mask, diff_lambda, subln_g,
                                  qa, kva, swa_sinks, swa_bias, l)
        x, w_in_b = _mix_ffn(out_a, out_d, sg, x, mod, w_bs, w_bd, w_o, n2g, w_up_b, conv_w, conv_b3, w_down_b,
                             final_g, w_in, l, final=(l == depth - 1))
    return x
```
